```python
import math
import jax, jax.numpy as jnp
from jax import lax
import numpy as np

D_MODEL = 1024
BATCH = 8
SEQ = 4096
DEPTH = 2

D_MIX = D_MODEL
SSD_INNER = D_MODEL // 2
SSD_HEAD_DIM = 64
SSD_HEADS = SSD_INNER // SSD_HEAD_DIM
SSD_GROUPS = 2
SSD_HEADS_PER_GROUP = SSD_HEADS // SSD_GROUPS
SSD_STATE = 64
SSD_CONV = 4
SSD_CHUNK = 128
SSD_CONV_DIM = SSD_INNER + 2 * SSD_GROUPS * SSD_STATE
DT_MIN = 0.001
DT_MAX = 0.1
MLA_HEADS = 4
MLA_Q_RANK = D_MODEL // 4
MLA_KV_RANK = D_MODEL // 8
MLA_NOPE = 64
MLA_ROPE = 32
MLA_V = 64
MLA_OUT = MLA_HEADS * MLA_V
MLA_SCALE = (MLA_NOPE + MLA_ROPE) ** -0.5
MLA_Q_BLOCK = 128
ROPE_THETA = 10000.0
GM_GROUPS = 4
GM_GROUP_DIM = 64
GM_WIDTH = GM_GROUPS * GM_GROUP_DIM
GM_CHUNK = 128
IN_SIZES = (SSD_INNER, SSD_INNER, SSD_GROUPS * SSD_STATE, SSD_GROUPS * SSD_STATE, SSD_HEADS,
            MLA_Q_RANK, MLA_KV_RANK, MLA_ROPE, GM_WIDTH, GM_WIDTH)
D_IN = sum(IN_SIZES)
D_FF = 2816
N_EXPERTS = 8
TOP_K = 2
DN_ALPHA = (2 * DEPTH) ** 0.25
DN_BETA = (8 * DEPTH) ** -0.25
LN_EPS = 1e-5
RMS_EPS = 1e-6
ADA_SCALE = 0.3

kernel_name = "hymba_ssd_mla_gmlp_moe_deepnorm_adaln"


def layer_norm(x, g, b):
    xf = x.astype(jnp.float32)
    mu = jnp.mean(xf, axis=-1, keepdims=True)
    var = jnp.mean(jnp.square(xf - mu), axis=-1, keepdims=True)
    return ((xf - mu) * lax.rsqrt(var + LN_EPS)).astype(x.dtype) * g + b


def rms_norm(x, g):
    xf = x.astype(jnp.float32)
    return (xf * lax.rsqrt(jnp.mean(jnp.square(xf), axis=-1, keepdims=True) + RMS_EPS)).astype(x.dtype) * g


def modulation(c, w, b):
    m = jnp.einsum("bd,de->be", jax.nn.silu(c), w) + b
    shift, scale, gate = jnp.split(m, 3, axis=-1)
    return shift[:, None], scale[:, None], gate[:, None]


def causal_depthwise_conv(x, w, b):
    k, ch = w.shape
    y = lax.conv_general_dilated(x, w[:, None, :], window_strides=(1,), padding=[(k - 1, 0)],
                                 dimension_numbers=("NWC", "WIO", "NWC"), feature_group_count=ch)
    return y + b


def segsum(a):
    n = a.shape[-1]
    cs = jnp.cumsum(a, axis=-1)
    diff = cs[..., :, None] - cs[..., None, :]
    return jnp.where(jnp.tril(jnp.ones((n, n), dtype=bool)), diff, -jnp.inf)


def ssd_scan(xs, bm, cm, dt, a_log, d_skip):
    b, t, _ = xs.shape
    nc, l = t // SSD_CHUNK, SSD_CHUNK
    g, r, p, n = SSD_GROUPS, SSD_HEADS_PER_GROUP, SSD_HEAD_DIM, SSD_STATE
    x = xs.reshape(b, nc, l, g, r, p)
    bc = bm.reshape(b, nc, l, g, n).astype(jnp.float32)
    cc = cm.reshape(b, nc, l, g, n).astype(jnp.float32)
    dtc = dt.reshape(b, nc, l, g, r)
    a_head = -jnp.exp(a_log.astype(jnp.float32)).reshape(g, r)
    a = jnp.transpose(dtc * a_head, (0, 3, 4, 1, 2))
    a_cs = jnp.cumsum(a, axis=-1)
    xdt = x.astype(jnp.float32) * dtc[..., None]
    decay_in = jnp.exp(segsum(a))
    cb = jnp.einsum("bclgn,bcsgn->bcgls", cc, bc)
    y_diag = jnp.einsum("bcgls,bgrcls,bcsgrp->bclgrp", cb, decay_in, xdt)
    decay_states = jnp.exp(a_cs[..., -1:] - a_cs)
    states = jnp.einsum("bcsgn,bgrcs,bcsgrp->bcgrpn", bc, decay_states, xdt)
    states = jnp.concatenate([jnp.zeros_like(states[:, :1]), states], axis=1)
    chunk_a = jnp.pad(a_cs[..., -1], ((0, 0), (0, 0), (0, 0), (1, 0)))
    decay_chunk = jnp.exp(segsum(chunk_a))
    states = jnp.einsum("bgrzc,bcgrpn->bzgrpn", decay_chunk, states)[:, :-1]
    y_off = jnp.einsum("bclgn,bcgrpn,bgrcl->bclgrp", cc, states, jnp.exp(a_cs))
    y = y_diag + y_off + x.astype(jnp.float32) * d_skip.astype(jnp.float32).reshape(g, r)[:, :, None]
    return y.reshape(b, t, SSD_INNER).astype(xs.dtype)


def gated_group_rmsnorm(y, z, w):
    gy = y * jax.nn.silu(z)
    b, t, _ = gy.shape
    gf = gy.reshape(b, t, SSD_GROUPS, SSD_INNER // SSD_GROUPS).astype(jnp.float32)
    gf = gf * lax.rsqrt(jnp.mean(jnp.square(gf), axis=-1, keepdims=True) + RMS_EPS)
    return gf.reshape(b, t, SSD_INNER).astype(y.dtype) * w


def rope_tables(t):
    inv = ROPE_THETA ** (-jnp.arange(0, MLA_ROPE, 2, dtype=jnp.float32) / MLA_ROPE)
    ang = jnp.arange(t, dtype=jnp.float32)[:, None] * inv[None, :]
    return jnp.cos(ang), jnp.sin(ang)


def apply_rope(x, cos, sin):
    x1, x2 = jnp.split(x.astype(jnp.float32), 2, axis=-1)
    return jnp.concatenate([x1 * cos - x2 * sin, x1 * sin + x2 * cos], axis=-1).astype(x.dtype)


def mla_attention(q_nope, q_pe, k_nope, k_pe, v):
    b, t, h, _ = q_nope.shape
    nb = t // MLA_Q_BLOCK
    qn = jnp.moveaxis(q_nope.reshape(b, nb, MLA_Q_BLOCK, h, MLA_NOPE), 1, 0)
    qp = jnp.moveaxis(q_pe.reshape(b, nb, MLA_Q_BLOCK, h, MLA_ROPE), 1, 0)
    kpos = jnp.arange(t)

    def block(args):
        qn_b, qp_b, i = args
        s = (jnp.einsum("bqhd,bkhd->bhqk", qn_b, k_nope)
             + jnp.einsum("bqhd,bkd->bhqk", qp_b, k_pe)).astype(jnp.float32) * MLA_SCALE
        qpos = i * MLA_Q_BLOCK + jnp.arange(MLA_Q_BLOCK)
        s = jnp.where(kpos[None, :] <= qpos[:, None], s, -jnp.inf)
        pr = jax.nn.softmax(s, axis=-1).astype(v.dtype)
        return jnp.einsum("bhqk,bkhd->bqhd", pr, v)

    out = lax.map(block, (qn, qp, jnp.arange(nb)))
    return jnp.moveaxis(out, 0, 1).reshape(b, t, h * MLA_V)


def spatial_gating(gu, gv, ln_g, ln_b, w_s, b_s):
    gu = jax.nn.gelu(gu, approximate=False)
    gv = layer_norm(jax.nn.gelu(gv, approximate=False), ln_g, ln_b)
    b, t, _ = gv.shape
    vc = gv.reshape(b, t // GM_CHUNK, GM_CHUNK, GM_GROUPS, GM_GROUP_DIM)
    w = w_s * jnp.tril(jnp.ones((GM_CHUNK, GM_CHUNK), dtype=w_s.dtype))
    s = jnp.einsum("gts,bcsgd->bctgd", w, vc) + jnp.transpose(b_s)[:, :, None]
    return gu * s.reshape(b, t, GM_WIDTH)


def hybrid_mixer(h, cos, sin, w_in, conv_w, conv_b, dt_bias, a_log, d_skip, ssd_norm_w,
                 q_norm, w_qb, kv_norm, w_kvb, gm_ln_g, gm_ln_b, gm_w_s, gm_b_s, w_out):
    b, t, _ = h.shape
    proj = jnp.einsum("btd,de->bte", h, w_in)
    split_at = np.cumsum(IN_SIZES)[:-1].tolist()
    z, xs, bm, cm, dt, q_lat, kv_lat, k_pe, gu, gv = jnp.split(proj, split_at, axis=-1)
    xbc = jax.nn.silu(causal_depthwise_conv(jnp.concatenate([xs, bm, cm], axis=-1), conv_w, conv_b))
    xs, bm, cm = jnp.split(xbc, [SSD_INNER, SSD_INNER + SSD_GROUPS * SSD_STATE], axis=-1)
    dt = jax.nn.softplus(dt.astype(jnp.float32) + dt_bias.astype(jnp.float32))
    y_ssd = gated_group_rmsnorm(ssd_scan(xs, bm, cm, dt, a_log, d_skip), z, ssd_norm_w)
    q = jnp.einsum("btr,re->bte", rms_norm(q_lat, q_norm), w_qb).reshape(b, t, MLA_HEADS, MLA_NOPE + MLA_ROPE)
    q_nope, q_pe = jnp.split(q, [MLA_NOPE], axis=-1)
    q_pe = apply_rope(q_pe, cos[:, None, :], sin[:, None, :])
    kv = jnp.einsum("btr,re->bte", rms_norm(kv_lat, kv_norm), w_kvb).reshape(b, t, MLA_HEADS, MLA_NOPE + MLA_V)
    k_nope, v_heads = jnp.split(kv, [MLA_NOPE], axis=-1)
    k_pe = apply_rope(k_pe, cos, sin)
    y_mla = mla_attention(q_nope, q_pe, k_nope, k_pe, v_heads)
    y_gm = spatial_gating(gu, gv, gm_ln_g, gm_ln_b, gm_w_s, gm_b_s)
    y = jnp.concatenate([y_ssd, y_mla.astype(y_ssd.dtype), y_gm.astype(y_ssd.dtype)], axis=-1)
    return jnp.einsum("bte,ed->btd", y, w_out)


def swiglu(h, w1, w3, w2):
    a = jnp.einsum("btd,df->btf", h, w1)
    g = jnp.einsum("btd,df->btf", h, w3)
    return jnp.einsum("btf,fd->btd", jax.nn.silu(a) * g, w2)


def moe_swiglu(h, w_router, w1, w3, w2):
    logits = jnp.einsum("btd,de->bte", h, w_router).astype(jnp.float32)
    top_v, top_i = lax.top_k(logits, TOP_K)
    gates = jax.nn.softmax(top_v, axis=-1)
    combine = jnp.sum(jax.nn.one_hot(top_i, N_EXPERTS, dtype=jnp.float32) * gates[..., None], axis=-2)
    combine = combine.astype(h.dtype)
    out = jnp.zeros_like(h)
    for e in range(N_EXPERTS):
        out = out + combine[..., e:e + 1] * swiglu(h, w1[e], w3[e], w2[e])
    return out


def setup_inputs(seed: int = 0) -> dict:
    key = jax.random.key(seed)
    ks = iter(jax.random.split(key, 40))
    f32 = jnp.float32

    def nrm(shape, scale):
        return jax.random.normal(next(ks), shape, f32) * scale

    def gain(shape):
        return 1.0 + nrm(shape, 0.02)

    def bias(shape):
        return nrm(shape, 0.02)

    n_dense = (DEPTH + 1) // 2
    n_moe = DEPTH // 2
    u = jax.random.uniform(next(ks), (DEPTH, SSD_HEADS), f32)
    dt0 = jnp.exp(u * (math.log(DT_MAX) - math.log(DT_MIN)) + math.log(DT_MIN))
    dt_bias = dt0 + jnp.log(-jnp.expm1(-dt0))
    a_log = jnp.log(jax.random.uniform(next(ks), (DEPTH, SSD_HEADS), f32, 1.0, 16.0))
    return {
        "x": nrm((BATCH, SEQ, D_MODEL), 1.0),
        "c": nrm((BATCH, D_MODEL), 1.0),
        "ln0_g": gain((D_MODEL,)),
        "ln0_b": bias((D_MODEL,)),
        "ada_w": nrm((DEPTH, 2, D_MODEL, 3 * D_MODEL), ADA_SCALE * D_MODEL ** -0.5),
        "ada_b": bias((DEPTH, 2, 3 * D_MODEL)),
        "post_ln_g": gain((DEPTH, 2, D_MODEL)),
        "post_ln_b": bias((DEPTH, 2, D_MODEL)),
        "w_in": nrm((DEPTH, D_MODEL, D_IN), D_MODEL ** -0.5),
        "ssd_conv_w": nrm((DEPTH, SSD_CONV, SSD_CONV_DIM), SSD_CONV ** -0.5),
        "ssd_conv_b": bias((DEPTH, SSD_CONV_DIM)),
        "ssd_dt_bias": dt_bias,
        "ssd_a_log": a_log,
        "ssd_d": gain((DEPTH, SSD_HEADS)),
        "ssd_norm_w": gain((DEPTH, SSD_INNER)),
        "mla_q_norm": gain((DEPTH, MLA_Q_RANK)),
        "mla_w_qb": nrm((DEPTH, MLA_Q_RANK, MLA_HEADS * (MLA_NOPE + MLA_ROPE)), MLA_Q_RANK ** -0.5),
        "mla_kv_norm": gain((DEPTH, MLA_KV_RANK)),
        "mla_w_kvb": nrm((DEPTH, MLA_KV_RANK, MLA_HEADS * (MLA_NOPE + MLA_V)), MLA_KV_RANK ** -0.5),
        "gm_ln_g": gain((DEPTH, GM_WIDTH)),
        "gm_ln_b": bias((DEPTH, GM_WIDTH)),
        "gm_w_s": nrm((DEPTH, GM_GROUPS, GM_CHUNK, GM_CHUNK), GM_CHUNK ** -0.5),
        "gm_b_s": 1.0 + nrm((DEPTH, GM_GROUPS, GM_CHUNK), 0.1),
        "w_out": nrm((DEPTH, D_MIX, D_MODEL), DN_BETA * D_MIX ** -0.5),
        "ffn_w1": nrm((n_dense, D_MODEL, D_FF), D_MODEL ** -0.5),
        "ffn_w3": nrm((n_dense, D_MODEL, D_FF), D_MODEL ** -0.5),
        "ffn_w2": nrm((n_dense, D_FF, D_MODEL), DN_BETA * D_FF ** -0.5),
        "moe_router": nrm((n_moe, D_MODEL, N_EXPERTS), D_MODEL ** -0.5),
        "moe_w1": nrm((n_moe, N_EXPERTS, D_MODEL, D_FF), D_MODEL ** -0.5),
        "moe_w3": nrm((n_moe, N_EXPERTS, D_MODEL, D_FF), D_MODEL ** -0.5),
        "moe_w2": nrm((n_moe, N_EXPERTS, D_FF, D_MODEL), DN_BETA * D_FF ** -0.5),
    }


def reference(x, c, ln0_g, ln0_b, ada_w, ada_b, post_ln_g, post_ln_b, w_in, ssd_conv_w, ssd_conv_b,
              ssd_dt_bias, ssd_a_log, ssd_d, ssd_norm_w, mla_q_norm, mla_w_qb, mla_kv_norm, mla_w_kvb,
              gm_ln_g, gm_ln_b, gm_w_s, gm_b_s, w_out, ffn_w1, ffn_w3, ffn_w2,
              moe_router, moe_w1, moe_w3, moe_w2):
    cos, sin = rope_tables(x.shape[1])
    x = layer_norm(x, ln0_g, ln0_b)
    for layer in range(DEPTH):
        shift, scale, gate = modulation(c, ada_w[layer, 0], ada_b[layer, 0])
        h = x * (1.0 + scale) + shift
        y = hybrid_mixer(h, cos, sin, w_in[layer], ssd_conv_w[layer], ssd_conv_b[layer],
                         ssd_dt_bias[layer], ssd_a_log[layer], ssd_d[layer], ssd_norm_w[layer],
                         mla_q_norm[layer], mla_w_qb[layer], mla_kv_norm[layer], mla_w_kvb[layer],
                         gm_ln_g[layer], gm_ln_b[layer], gm_w_s[layer], gm_b_s[layer], w_out[layer])
        x = layer_norm(DN_ALPHA * x + (1.0 + gate) * y, post_ln_g[layer, 0], post_ln_b[layer, 0])
        shift, scale, gate = modulation(c, ada_w[layer, 1], ada_b[layer, 1])
        h = x * (1.0 + scale) + shift
        i = layer // 2
        if layer % 2 == 0:
            y = swiglu(h, ffn_w1[i], ffn_w3[i], ffn_w2[i])
        else:
            y = moe_swiglu(h, moe_router[i], moe_w1[i], moe_w3[i], moe_w2[i])
        x = layer_norm(DN_ALPHA * x + (1.0 + gate) * y, post_ln_g[layer, 1], post_ln_b[layer, 1])
    return x
```

```python
import functools
import math

import numpy as np
import jax
import jax.numpy as jnp
from jax import lax
from jax.experimental import pallas as pl
from jax.experimental.pallas import tpu as pltpu

F32 = jnp.float32
BF16 = jnp.bfloat16

D_MODEL = 1024
DEPTH = 2
SSD_INNER = 512
SSD_HEAD_DIM = 64
SSD_HEADS = 8
SSD_GROUPS = 2
SSD_STATE = 64
SSD_CONV = 4
CHUNK = 128
SSD_CONV_DIM = 768
MLA_HEADS = 4
MLA_Q_RANK = 256
MLA_KV_RANK = 128
MLA_NOPE = 64
MLA_ROPE = 32
MLA_V = 64
MLA_SCALE = (MLA_NOPE + MLA_ROPE) ** -0.5
ROPE_THETA = 10000.0
GM_GROUPS = 4
GM_GROUP_DIM = 64
GM_WIDTH = 256
D_FF = 2816
N_EXPERTS = 8
DN_ALPHA = (2 * DEPTH) ** 0.25
LN_EPS = 1e-5
RMS_EPS = 1e-6

LANES = 128
FF_CHUNK = 256
VMEM_LIMIT = 56 * 1024 * 1024

MISC_KPE = 0
MISC_KPE_SW = 32
MISC_DT = 64
D_IN_PAD = 1280 + 384 + 512 + 128


def _cparams(sem):
    return pltpu.CompilerParams(dimension_semantics=sem, vmem_limit_bytes=VMEM_LIMIT)


def _silu(x):
    return x * jax.nn.sigmoid(x)


def _gelu(x):
    return 0.5 * x * (1.0 + lax.erf(x * (2.0 ** -0.5)))


def _softplus(x):
    return jnp.maximum(x, 0.0) + jnp.log1p(jnp.exp(-jnp.abs(x)))


def _layer_norm(x, g, b):
    mu = jnp.mean(x, axis=-1, keepdims=True)
    xc = x - mu
    var = jnp.mean(xc * xc, axis=-1, keepdims=True)
    return xc * lax.rsqrt(var + LN_EPS) * g + b


def _rms(x):
    return x * lax.rsqrt(jnp.mean(x * x, axis=-1, keepdims=True) + RMS_EPS)


def _dot(a, b):
    return jnp.dot(a, b, preferred_element_type=F32)


def _dot_nt(a, b):
    return lax.dot_general(a, b, (((1,), (1,)), ((), ())), preferred_element_type=F32)


def _dot_tn(a, b):
    return lax.dot_general(a, b, (((0,), (0,)), ((), ())), preferred_element_type=F32)


def _mod_kernel(c_ref, w_ref, b_ref, o_ref):
    s = _silu(c_ref[...]).astype(BF16)
    o_ref[0] = _dot(s, w_ref[0].astype(BF16)) + b_ref[0]


def _modulations(c, ada_w, ada_b):
    nb = c.shape[0]
    n = ada_w.shape[0] * ada_w.shape[1]
    w = ada_w.reshape(n, D_MODEL, 3 * D_MODEL)
    b = ada_b.reshape(n, 1, 3 * D_MODEL)
    tn = 1024
    return pl.pallas_call(
        _mod_kernel,
        grid=(n, 3 * D_MODEL // tn),
        in_specs=[
            pl.BlockSpec((nb, D_MODEL), lambda j, k: (0, 0)),
            pl.BlockSpec((1, D_MODEL, tn), lambda j, k: (j, 0, k)),
            pl.BlockSpec((1, 1, tn), lambda j, k: (j, 0, k)),
        ],
        out_specs=pl.BlockSpec((1, nb, tn), lambda j, k: (j, 0, k)),
        out_shape=jax.ShapeDtypeStruct((n, nb, 3 * D_MODEL), F32),
        compiler_params=_cparams(("arbitrary", "arbitrary")),
        name="adaln_mod",
    )(c, w, b)


def _inproj_kernel(*refs, do_ln):
    if do_ln:
        x_ref, mod_ref, g_ref, b_ref, w_ref, xo_ref, ssd_ref, mla_ref, gm_ref, misc_ref = refs
    else:
        x_ref, mod_ref, w_ref, ssd_ref, mla_ref, gm_ref, misc_ref = refs
    x = x_ref[0]
    if do_ln:
        x = _layer_norm(x, g_ref[...], b_ref[...])
        xo_ref[0] = x
    m = mod_ref[0]
    h = x * (1.0 + m[:, D_MODEL:2 * D_MODEL]) + m[:, 0:D_MODEL]
    p = _dot(h.astype(BF16), w_ref[...])
    ssd_ref[0] = p[:, 0:1280].astype(BF16)
    mla_ref[0] = p[:, 1280:1664].astype(BF16)
    gm_ref[0] = p[:, 1664:2176].astype(BF16)
    misc_ref[0] = p[:, 2176:2304]


def _inproj(x, mod, w_pad, ln_g=None, ln_b=None, *, tm):
    nb, t, _ = x.shape
    do_ln = ln_g is not None
    row = lambda b, i: (b, i, 0)
    const2 = lambda b, i: (0, 0)
    in_specs = [pl.BlockSpec((1, tm, D_MODEL), row), pl.BlockSpec((1, 1, 3 * D_MODEL), lambda b, i: (b, 0, 0))]
    args = [x, mod]
    if do_ln:
        in_specs += [pl.BlockSpec((1, D_MODEL), const2), pl.BlockSpec((1, D_MODEL), const2)]
        args += [ln_g.reshape(1, D_MODEL), ln_b.reshape(1, D_MODEL)]
    in_specs.append(pl.BlockSpec((D_MODEL, D_IN_PAD), const2))
    args.append(w_pad)
    out_specs = [pl.BlockSpec((1, tm, 1280), row), pl.BlockSpec((1, tm, 384), row),
                 pl.BlockSpec((1, tm, 512), row), pl.BlockSpec((1, tm, LANES), row)]
    out_shape = [jax.ShapeDtypeStruct((nb, t, 1280), BF16), jax.ShapeDtypeStruct((nb, t, 384), BF16),
                 jax.ShapeDtypeStruct((nb, t, 512), BF16), jax.ShapeDtypeStruct((nb, t, LANES), F32)]
    if do_ln:
        out_specs = [pl.BlockSpec((1, tm, D_MODEL), row)] + out_specs
        out_shape = [jax.ShapeDtypeStruct((nb, t, D_MODEL), F32)] + out_shape
    return pl.pallas_call(
        functools.partial(_inproj_kernel, do_ln=do_ln),
        grid=(nb, t // tm),
        in_specs=in_specs, out_specs=out_specs, out_shape=out_shape,
        compiler_params=_cparams(("arbitrary", "arbitrary")),
        name="inproj_ln" if do_ln else "inproj",
    )(*args)


def _ssd_kernel(ssd_ref, misc_ref, cw_ref, cb_ref, dtb_ref, arow_ref, drow_ref, nw_ref, o_ref,
                xpad_ref, st_ref, y_ref):
    L = CHUNK

    @pl.when(pl.program_id(1) == 0)
    def _():
        xpad_ref[0:8, :] = jnp.zeros((8, SSD_CONV_DIM), F32)
        st_ref[...] = jnp.zeros(st_ref.shape, F32)

    blk = ssd_ref[0]
    z = blk[:, 0:SSD_INNER].astype(F32)
    xpad_ref[8:8 + L, :] = blk[:, SSD_INNER:SSD_INNER + SSD_CONV_DIM].astype(F32)
    cw = cw_ref[...]
    acc = cb_ref[...] + cw[3:4, :] * xpad_ref[8:8 + L, :]
    for k in range(SSD_CONV - 1):
        acc = acc + cw[k:k + 1, :] * xpad_ref[pl.ds(5 + k, L), :]
    xpad_ref[0:8, :] = xpad_ref[L:L + 8, :]
    xbc = _silu(acc)
    xs = xbc[:, 0:SSD_INNER]
    bm = xbc[:, SSD_INNER:SSD_INNER + 128]
    cm = xbc[:, SSD_INNER + 128:SSD_INNER + 256]

    dtv = _softplus(misc_ref[0] + dtb_ref[...])
    acs = dtv * arow_ref[...]
    row = lax.broadcasted_iota(jnp.int32, (L, LANES), 0)
    col = lax.broadcasted_iota(jnp.int32, (L, LANES), 1)
    s = 1
    while s < L:
        acs = acs + jnp.where(row >= s, pltpu.roll(acs, s, 0), 0.0)
        s *= 2
    acs_t = acs.T
    atot = acs[L - 1:L, :]
    causal = row >= col

    for g in range(SSD_GROUPS):
        bg = bm[:, 64 * g:64 * g + 64].astype(BF16)
        cg = cm[:, 64 * g:64 * g + 64].astype(BF16)
        cb = _dot_nt(cg, bg)
        for r in range(SSD_HEADS // SSD_GROUPS):
            h = g * (SSD_HEADS // SSD_GROUPS) + r
            lane = MISC_DT + h
            a_col = acs[:, lane:lane + 1]
            a_row = acs_t[lane:lane + 1, :]
            decay = jnp.exp(jnp.where(causal, a_col - a_row, -jnp.inf))
            x_h = xs[:, 64 * h:64 * h + 64]
            xdt = x_h * dtv[:, lane:lane + 1]
            y = _dot((cb * decay).astype(BF16), xdt.astype(BF16))
            st = st_ref[h]
            y = y + jnp.exp(a_col) * _dot(cg, st.astype(BF16))
            y = y + x_h * drow_ref[:, 64 * h:64 * h + 64]
            tot = atot[:, lane:lane + 1]
            wst = (xdt * jnp.exp(tot - a_col)).astype(BF16)
            st_ref[h] = jnp.exp(tot) * st + _dot_tn(bg, wst)
            y_ref[:, 64 * h:64 * h + 64] = y

    gy = y_ref[...] * _silu(z)
    half = SSD_INNER // SSD_GROUPS
    for g in range(SSD_GROUPS):
        seg = gy[:, half * g:half * g + half]
        o_ref[0, :, half * g:half * g + half] = (_rms(seg) * nw_ref[:, half * g:half * g + half]).astype(BF16)


def _ssd(ssd_in, misc, conv_w, conv_b, dt_bias, a_log, d_skip, norm_w):
    nb, t, _ = ssd_in.shape
    lane_pad = (MISC_DT, LANES - MISC_DT - SSD_HEADS)
    dtb = jnp.pad(dt_bias.astype(F32), lane_pad).reshape(1, LANES)
    arow = jnp.pad(-jnp.exp(a_log.astype(F32)), lane_pad).reshape(1, LANES)
    drow = jnp.repeat(d_skip.astype(F32), SSD_HEAD_DIM).reshape(1, SSD_INNER)
    row = lambda b, i: (b, i, 0)
    const2 = lambda b, i: (0, 0)
    return pl.pallas_call(
        _ssd_kernel,
        grid=(nb, t // CHUNK),
        in_specs=[
            pl.BlockSpec((1, CHUNK, 1280), row),
            pl.BlockSpec((1, CHUNK, LANES), row),
            pl.BlockSpec((SSD_CONV, SSD_CONV_DIM), const2),
            pl.BlockSpec((1, SSD_CONV_DIM), const2),
            pl.BlockSpec((1, LANES), const2),
            pl.BlockSpec((1, LANES), const2),
            pl.BlockSpec((1, SSD_INNER), const2),
            pl.BlockSpec((1, SSD_INNER), const2),
        ],
        out_specs=pl.BlockSpec((1, CHUNK, SSD_INNER), row),
        out_shape=jax.ShapeDtypeStruct((nb, t, SSD_INNER), BF16),
        scratch_shapes=[
            pltpu.VMEM((CHUNK + 8, SSD_CONV_DIM), F32),
            pltpu.VMEM((SSD_HEADS, SSD_STATE, SSD_HEAD_DIM), F32),
            pltpu.VMEM((CHUNK, SSD_INNER), F32),
        ],
        compiler_params=_cparams(("arbitrary", "arbitrary")),
        name="ssd_scan",
    )(ssd_in, misc, conv_w, conv_b.reshape(1, -1), dtb, arow, drow, norm_w.reshape(1, -1))


def _gmlp_kernel(gm_ref, lg_ref, lb_ref, ws_ref, bs_ref, o_ref):
    blk = gm_ref[0].astype(F32)
    gu = _gelu(blk[:, 0:GM_WIDTH])
    gv = _layer_norm(_gelu(blk[:, GM_WIDTH:2 * GM_WIDTH]), lg_ref[...], lb_ref[...]).astype(BF16)
    row = lax.broadcasted_iota(jnp.int32, (CHUNK, CHUNK), 0)
    col = lax.broadcasted_iota(jnp.int32, (CHUNK, CHUNK), 1)
    for g in range(GM_GROUPS):
        lo = GM_GROUP_DIM * g
        w = jnp.where(row >= col, ws_ref[g], 0.0).astype(BF16)
        s = _dot(w, gv[:, lo:lo + GM_GROUP_DIM]) + bs_ref[:, lo:lo + GM_GROUP_DIM]
        o_ref[0, :, lo:lo + GM_GROUP_DIM] = (gu[:, lo:lo + GM_GROUP_DIM] * s).astype(BF16)


def _gmlp(gm_in, ln_g, ln_b, w_s, b_s):
    nb, t, _ = gm_in.shape
    bs = jnp.repeat(jnp.transpose(b_s), GM_GROUP_DIM, axis=1)
    row = lambda b, i: (b, i, 0)
    return pl.pallas_call(
        _gmlp_kernel,
        grid=(nb, t // CHUNK),
        in_specs=[
            pl.BlockSpec((1, CHUNK, 2 * GM_WIDTH), row),
            pl.BlockSpec((1, GM_WIDTH), lambda b, i: (0, 0)),
            pl.BlockSpec((1, GM_WIDTH), lambda b, i: (0, 0)),
            pl.BlockSpec((GM_GROUPS, CHUNK, CHUNK), lambda b, i: (0, 0, 0)),
            pl.BlockSpec((CHUNK, GM_WIDTH), lambda b, i: (0, 0)),
        ],
        out_specs=pl.BlockSpec((1, CHUNK, GM_WIDTH), row),
        out_shape=jax.ShapeDtypeStruct((nb, t, GM_WIDTH), BF16),
        compiler_params=_cparams(("arbitrary", "arbitrary")),
        name="gmlp_gate",
    )(gm_in, ln_g.reshape(1, -1), ln_b.reshape(1, -1), w_s, bs)


def _mla_prep_kernel(mla_ref, misc_ref, qn_ref, kvn_ref, wqa_ref, wqb_ref, wk_ref, wv_ref,
                     cq_ref, sq_ref, ck_ref, sk_ref, q_ref, k_ref, v_ref):
    m = mla_ref[0].astype(F32)
    qn = (_rms(m[:, 0:MLA_Q_RANK]) * qn_ref[...]).astype(BF16)
    kvn = (_rms(m[:, MLA_Q_RANK:MLA_Q_RANK + MLA_KV_RANK]) * kvn_ref[...]).astype(BF16)
    qa = _dot(qn, wqa_ref[...])
    qb = _dot(qn, wqb_ref[...])
    kk = _dot(kvn, wk_ref[...])
    vv = _dot(kvn, wv_ref[...])
    misc = misc_ref[0]
    kr = pltpu.roll(misc, 64 - MISC_KPE, 1) * ck_ref[...] + pltpu.roll(misc, 64 - MISC_KPE_SW, 1) * sk_ref[...]
    cq = cq_ref[...]
    sq = sq_ref[...]
    for h in range(MLA_HEADS):
        lo = LANES * h
        q_ref[0, h] = (qa[:, lo:lo + LANES] * cq + qb[:, lo:lo + LANES] * sq).astype(BF16)
        k_ref[0, h] = (kk[:, lo:lo + LANES] + kr).astype(BF16)
        v_ref[0, h] = vv[:, MLA_V * h:MLA_V * h + MLA_V].astype(BF16)


def _rope_tables(t):
    inv = ROPE_THETA ** (-jnp.arange(0, MLA_ROPE, 2, dtype=F32) / MLA_ROPE)
    ang = jnp.arange(t, dtype=F32)[:, None] * inv[None, :]
    cos, sin = jnp.cos(ang), jnp.sin(ang)
    z64 = jnp.zeros((t, 64), F32)
    z32 = jnp.zeros((t, 32), F32)
    one64 = jnp.ones((t, 64), F32)
    cq = jnp.concatenate([one64, cos, cos, z32], axis=1) * MLA_SCALE
    sq = jnp.concatenate([z64, -sin, sin, z32], axis=1) * MLA_SCALE
    ck = jnp.concatenate([z64, cos, cos, z32], axis=1)
    sk = jnp.concatenate([z64, -sin, sin, z32], axis=1)
    return cq, sq, ck, sk


def _mla_weights(w_qb, w_kvb):
    dq = MLA_NOPE + MLA_ROPE
    wq = w_qb.reshape(MLA_Q_RANK, MLA_HEADS, dq)
    nope, pe = wq[..., :MLA_NOPE], wq[..., MLA_NOPE:]
    pe_sw = jnp.concatenate([pe[..., MLA_ROPE // 2:], pe[..., :MLA_ROPE // 2]], axis=-1)
    zpad = jnp.zeros((MLA_Q_RANK, MLA_HEADS, LANES - dq), w_qb.dtype)
    wqa = jnp.concatenate([nope, pe, zpad], axis=-1).reshape(MLA_Q_RANK, MLA_HEADS * LANES)
    wqb = jnp.concatenate([jnp.zeros_like(nope), pe_sw, zpad], axis=-1).reshape(MLA_Q_RANK, MLA_HEADS * LANES)
    wkv = w_kvb.reshape(MLA_KV_RANK, MLA_HEADS, MLA_NOPE + MLA_V)
    wk = jnp.concatenate([wkv[..., :MLA_NOPE], jnp.zeros((MLA_KV_RANK, MLA_HEADS, LANES - MLA_NOPE), w_kvb.dtype)],
                         axis=-1).reshape(MLA_KV_RANK, MLA_HEADS * LANES)
    wv = wkv[..., MLA_NOPE:].reshape(MLA_KV_RANK, MLA_HEADS * MLA_V)
    return wqa.astype(BF16), wqb.astype(BF16), wk.astype(BF16), wv.astype(BF16)


def _mla_prep(mla_in, misc, q_norm, kv_norm, w_qb, w_kvb, tables, *, tt):
    nb, t, _ = mla_in.shape
    wqa, wqb, wk, wv = _mla_weights(w_qb, w_kvb)
    row = lambda b, i: (b, i, 0)
    const2 = lambda b, i: (0, 0)
    tab = pl.BlockSpec((tt, LANES), lambda b, i: (i, 0))
    hrow = lambda b, i: (b, 0, i, 0)
    return pl.pallas_call(
        _mla_prep_kernel,
        grid=(nb, t // tt),
        in_specs=[
            pl.BlockSpec((1, tt, 384), row),
            pl.BlockSpec((1, tt, LANES), row),
            pl.BlockSpec((1, MLA_Q_RANK), const2),
            pl.BlockSpec((1, MLA_KV_RANK), const2),
            pl.BlockSpec((MLA_Q_RANK, MLA_HEADS * LANES), const2),
            pl.BlockSpec((MLA_Q_RANK, MLA_HEADS * LANES), const2),
            pl.BlockSpec((MLA_KV_RANK, MLA_HEADS * LANES), const2),
            pl.BlockSpec((MLA_KV_RANK, MLA_HEADS * MLA_V), const2),
            tab, tab, tab, tab,
        ],
        out_specs=[
            pl.BlockSpec((1, MLA_HEADS, tt, LANES), hrow),
            pl.BlockSpec((1, MLA_HEADS, tt, LANES), hrow),
            pl.BlockSpec((1, MLA_HEADS, tt, MLA_V), hrow),
        ],
        out_shape=[
            jax.ShapeDtypeStruct((nb, MLA_HEADS, t, LANES), BF16),
            jax.ShapeDtypeStruct((nb, MLA_HEADS, t, LANES), BF16),
            jax.ShapeDtypeStruct((nb, MLA_HEADS, t, MLA_V), BF16),
        ],
        compiler_params=_cparams(("arbitrary", "arbitrary")),
        name="mla_prep",
    )(mla_in, misc, q_norm.reshape(1, -1), kv_norm.reshape(1, -1), wqa, wqb, wk, wv, *tables)


def _flash_kernel(q_ref, k_ref, v_ref, o_ref, *, tq):
    i = pl.program_id(1)
    row = lax.broadcasted_iota(jnp.int32, (tq, tq), 0)
    col = lax.broadcasted_iota(jnp.int32, (tq, tq), 1)
    outs = []
    for h in range(MLA_HEADS):
        q = q_ref[0, h]

        def step(j, carry, masked):
            m, l, acc = carry
            start = pl.multiple_of(j * tq, tq)
            kb = k_ref[0, h, pl.ds(start, tq), :]
            vb = v_ref[0, h, pl.ds(start, tq), :]
            s = _dot_nt(q, kb)
            if masked:
                s = jnp.where(col <= row, s, -jnp.inf)
            m_new = jnp.maximum(m, jnp.max(s, axis=1, keepdims=True))
            alpha = jnp.exp(m - m_new)
            p = jnp.exp(s - m_new)
            l = alpha * l + jnp.sum(p, axis=1, keepdims=True)
            acc = alpha * acc + _dot(p.astype(BF16), vb)
            return m_new, l, acc

        init = (jnp.full((tq, 1), -jnp.inf, F32), jnp.zeros((tq, 1), F32), jnp.zeros((tq, MLA_V), F32))
        carry = lax.fori_loop(0, i, functools.partial(step, masked=False), init)
        m, l, acc = step(i, carry, True)
        outs.append(acc / l)
    o_ref[0] = jnp.concatenate(outs, axis=-1).astype(BF16)


def _flash(q, k, v, *, tq):
    nb, nh, t, _ = q.shape
    return pl.pallas_call(
        functools.partial(_flash_kernel, tq=tq),
        grid=(nb, t // tq),
        in_specs=[
            pl.BlockSpec((1, nh, tq, LANES), lambda b, i: (b, 0, i, 0)),
            pl.BlockSpec((1, nh, t, LANES), lambda b, i: (b, 0, 0, 0)),
            pl.BlockSpec((1, nh, t, MLA_V), lambda b, i: (b, 0, 0, 0)),
        ],
        out_specs=pl.BlockSpec((1, tq, nh * MLA_V), lambda b, i: (b, i, 0)),
        out_shape=jax.ShapeDtypeStruct((nb, t, nh * MLA_V), BF16),
        compiler_params=_cparams(("arbitrary", "arbitrary")),
        name="mla_flash",
    )(q, k, v)


def _residual_ln(x, y, gate, g, b):
    return _layer_norm(DN_ALPHA * x + (1.0 + gate) * y, g, b)


def _outproj_kernel(*refs, route, tm):
    if route:
        (ys_ref, ym_ref, yg_ref, x_ref, mod1_ref, mod2_ref, w_ref, g_ref, b_ref, wr_ref,
         xo_ref, h_ref, rt_ref, cnt_ref, run_ref) = refs
    else:
        ys_ref, ym_ref, yg_ref, x_ref, mod1_ref, mod2_ref, w_ref, g_ref, b_ref, xo_ref, h_ref = refs
    y = (_dot(ys_ref[0], w_ref[0:512, :]) + _dot(ym_ref[0], w_ref[512:768, :])
         + _dot(yg_ref[0], w_ref[768:1024, :]))
    gate = mod1_ref[0][:, 2 * D_MODEL:3 * D_MODEL]
    xn = _residual_ln(x_ref[0], y, gate, g_ref[...], b_ref[...])
    xo_ref[0] = xn
    m2 = mod2_ref[0]
    h = xn * (1.0 + m2[:, D_MODEL:2 * D_MODEL]) + m2[:, 0:D_MODEL]
    if not route:
        h_ref[0] = h.astype(BF16)
        return
    h_ref[0] = h

    @pl.when((pl.program_id(0) == 0) & (pl.program_id(1) == 0))
    def _():
        run_ref[...] = jnp.zeros(run_ref.shape, F32)

    lane = lax.broadcasted_iota(jnp.int32, (tm, LANES), 1)
    logits = jnp.where(lane < N_EXPERTS, _dot(h.astype(BF16), wr_ref[...]), -jnp.inf)
    v1 = jnp.max(logits, axis=1, keepdims=True)
    i1 = jnp.min(jnp.where(logits == v1, lane, LANES), axis=1, keepdims=True)
    rest = jnp.where(lane == i1, -jnp.inf, logits)
    v2 = jnp.max(rest, axis=1, keepdims=True)
    i2 = jnp.min(jnp.where(rest == v2, lane, LANES), axis=1, keepdims=True)
    e2 = jnp.exp(v2 - v1)
    g1 = 1.0 / (1.0 + e2)
    g2 = e2 / (1.0 + e2)
    hot = ((lane == i1) | (lane == i2)).astype(F32)
    r_i = lax.broadcasted_iota(jnp.int32, (tm, tm), 0)
    c_i = lax.broadcasted_iota(jnp.int32, (tm, tm), 1)
    below = (c_i < r_i).astype(BF16)
    rank = _dot(below, hot.astype(BF16)) + run_ref[...]
    r1 = jnp.sum(jnp.where(lane == i1, rank, 0.0), axis=1, keepdims=True)
    r2 = jnp.sum(jnp.where(lane == i2, rank, 0.0), axis=1, keepdims=True)
    run = run_ref[...] + jnp.sum(hot, axis=0, keepdims=True)
    run_ref[...] = run
    cnt_ref[...] = jnp.broadcast_to(run, cnt_ref.shape)
    rt = jnp.where(lane == 0, i1.astype(F32), 0.0)
    rt = jnp.where(lane == 1, i2.astype(F32), rt)
    rt = jnp.where(lane == 2, r1, rt)
    rt = jnp.where(lane == 3, r2, rt)
    rt = jnp.where(lane == 4, g1, rt)
    rt = jnp.where(lane == 5, g2, rt)
    rt_ref[0] = rt


def _outproj(y_ssd, y_mla, y_gm, x, mod1, mod2, w_out, ln_g, ln_b, w_router=None, *, tm):
    nb, t, _ = x.shape
    route = w_router is not None
    row = lambda b, i: (b, i, 0)
    const2 = lambda b, i: (0, 0)
    modspec = pl.BlockSpec((1, 1, 3 * D_MODEL), lambda b, i: (b, 0, 0))
    in_specs = [
        pl.BlockSpec((1, tm, 512), row), pl.BlockSpec((1, tm, 256), row), pl.BlockSpec((1, tm, 256), row),
        pl.BlockSpec((1, tm, D_MODEL), row), modspec, modspec,
        pl.BlockSpec((D_MODEL, D_MODEL), const2),
        pl.BlockSpec((1, D_MODEL), const2), pl.BlockSpec((1, D_MODEL), const2),
    ]
    args = [y_ssd, y_mla, y_gm, x, mod1, mod2, w_out.astype(BF16), ln_g.reshape(1, -1), ln_b.reshape(1, -1)]
    out_specs = [pl.BlockSpec((1, tm, D_MODEL), row), pl.BlockSpec((1, tm, D_MODEL), row)]
    out_shape = [jax.ShapeDtypeStruct((nb, t, D_MODEL), F32),
                 jax.ShapeDtypeStruct((nb, t, D_MODEL), F32 if route else BF16)]
    scratch = []
    if route:
        wr = jnp.pad(w_router, ((0, 0), (0, LANES - N_EXPERTS))).astype(BF16)
        in_specs.append(pl.BlockSpec((D_MODEL, LANES), const2))
        args.append(wr)
        out_specs += [pl.BlockSpec((1, tm, LANES), row), pl.BlockSpec((8, LANES), const2)]
        out_shape += [jax.ShapeDtypeStruct((nb, t, LANES), F32), jax.ShapeDtypeStruct((8, LANES), F32)]
        scratch = [pltpu.VMEM((1, LANES), F32)]
    return pl.pallas_call(
        functools.partial(_outproj_kernel, route=route, tm=tm),
        grid=(nb, t // tm),
        in_specs=in_specs, out_specs=out_specs, out_shape=out_shape, scratch_shapes=scratch,
        compiler_params=_cparams(("arbitrary", "arbitrary")),
        name="outproj_route" if route else "outproj",
    )(*args)


def _swiglu_acc(h, w1_ref, w3_ref, w2_ref, widx):
    acc = None
    for j in range(D_FF // FF_CHUNK):
        lo = j * FF_CHUNK
        a = _dot(h, w1_ref[widx + (slice(None), slice(lo, lo + FF_CHUNK))])
        g = _dot(h, w3_ref[widx + (slice(None), slice(lo, lo + FF_CHUNK))])
        p = (_silu(a) * g).astype(BF16)
        part = _dot(p, w2_ref[widx + (slice(lo, lo + FF_CHUNK), slice(None))])
        acc = part if acc is None else acc + part
    return acc


def _ffn_kernel(h_ref, x_ref, mod_ref, w1_ref, w3_ref, w2_ref, g_ref, b_ref, o_ref):
    y = _swiglu_acc(h_ref[0], w1_ref, w3_ref, w2_ref, ())
    gate = mod_ref[0][:, 2 * D_MODEL:3 * D_MODEL]
    o_ref[0] = _residual_ln(x_ref[0], y, gate, g_ref[...], b_ref[...])


def _ffn(h, x, mod, w1, w3, w2, ln_g, ln_b, *, tm):
    nb, t, _ = x.shape
    row = lambda b, i: (b, i, 0)
    const2 = lambda b, i: (0, 0)
    wspec = lambda shape: pl.BlockSpec(shape, const2, pipeline_mode=pl.Buffered(1))
    return pl.pallas_call(
        _ffn_kernel,
        grid=(nb, t // tm),
        in_specs=[
            pl.BlockSpec((1, tm, D_MODEL), row), pl.BlockSpec((1, tm, D_MODEL), row),
            pl.BlockSpec((1, 1, 3 * D_MODEL), lambda b, i: (b, 0, 0)),
            wspec((D_MODEL, D_FF)), wspec((D_MODEL, D_FF)), wspec((D_FF, D_MODEL)),
            pl.BlockSpec((1, D_MODEL), const2), pl.BlockSpec((1, D_MODEL), const2),
        ],
        out_specs=pl.BlockSpec((1, tm, D_MODEL), row),
        out_shape=jax.ShapeDtypeStruct((nb, t, D_MODEL), F32),
        compiler_params=_cparams(("arbitrary", "arbitrary")),
        name="ffn_dense",
    )(h, x, mod, w1.astype(BF16), w3.astype(BF16), w2.astype(BF16), ln_g.reshape(1, -1), ln_b.reshape(1, -1))


def _dispatch_kernel(slot_ref, h_ref, hs_in_ref, hs_ref, sem, *, tm):
    del hs_in_ref
    base = pl.program_id(0) * (2 * tm)

    def row_copy(r, k):
        return pltpu.make_async_copy(h_ref.at[pl.ds(r, 1), :],
                                     hs_ref.at[pl.ds(slot_ref[base + 2 * r + k], 1), :], sem)

    def issue(r, c):
        row_copy(r, 0).start()
        row_copy(r, 1).start()
        return c

    lax.fori_loop(0, tm, issue, 0)

    def drain(r, c):
        row_copy(r, 0).wait()
        row_copy(r, 1).wait()
        return c

    lax.fori_loop(0, tm, drain, 0)


def _dispatch(slots, h, n_rows, *, tm):
    m = h.shape[0]
    hs0 = jnp.zeros((n_rows, D_MODEL), F32)
    return pl.pallas_call(
        functools.partial(_dispatch_kernel, tm=tm),
        grid_spec=pltpu.PrefetchScalarGridSpec(
            num_scalar_prefetch=1,
            grid=(m // tm,),
            in_specs=[pl.BlockSpec((tm, D_MODEL), lambda i, s: (i, 0)), pl.BlockSpec(memory_space=pl.ANY)],
            out_specs=pl.BlockSpec(memory_space=pl.ANY),
            scratch_shapes=[pltpu.SemaphoreType.DMA(())],
        ),
        out_shape=jax.ShapeDtypeStruct((n_rows, D_MODEL), F32),
        input_output_aliases={2: 0},
        compiler_params=_cparams(("arbitrary",)),
        name="moe_dispatch",
    )(slots, h, hs0)


def _moe_ffn_kernel(te_ref, na_ref, hs_ref, w1_ref, w3_ref, w2_ref, o_ref):
    active = pl.program_id(0) < na_ref[0]

    @pl.when(active)
    def _():
        o_ref[...] = _swiglu_acc(hs_ref[...].astype(BF16), w1_ref, w3_ref, w2_ref, (0,))

    @pl.when(jnp.logical_not(active))
    def _():
        o_ref[...] = jnp.zeros(o_ref.shape, F32)


def _moe_ffn(tile_expert, n_active, hs, w1, w3, w2, *, tg):
    n_rows = hs.shape[0]
    n_tiles = n_rows // tg
    rmap = lambda j, te, na: (jnp.minimum(j, na[0] - 1), 0)
    omap = lambda j, te, na: (j, 0)
    wmap = lambda j, te, na: (te[j], 0, 0)
    return pl.pallas_call(
        _moe_ffn_kernel,
        grid_spec=pltpu.PrefetchScalarGridSpec(
            num_scalar_prefetch=2,
            grid=(n_tiles,),
            in_specs=[
                pl.BlockSpec((tg, D_MODEL), rmap),
                pl.BlockSpec((1, D_MODEL, D_FF), wmap),
                pl.BlockSpec((1, D_MODEL, D_FF), wmap),
                pl.BlockSpec((1, D_FF, D_MODEL), wmap),
            ],
            out_specs=pl.BlockSpec((tg, D_MODEL), omap),
        ),
        out_shape=jax.ShapeDtypeStruct((n_rows, D_MODEL), F32),
        compiler_params=_cparams(("arbitrary",)),
        name="moe_ffn",
    )(tile_expert, n_active, hs, w1.astype(BF16), w3.astype(BF16), w2.astype(BF16))


def _combine_kernel(slot_ref, ys_ref, rt_ref, x_ref, mod_ref, g_ref, b_ref, o_ref, buf_ref, sem, *, tm):
    base = (pl.program_id(0) * pl.num_programs(1) + pl.program_id(1)) * (2 * tm)

    def row_copy(r, k):
        return pltpu.make_async_copy(ys_ref.at[pl.ds(slot_ref[base + 2 * r + k], 1), :],
                                     buf_ref.at[k, pl.ds(r, 1), :], sem)

    def issue(r, c):
        row_copy(r, 0).start()
        row_copy(r, 1).start()
        return c

    lax.fori_loop(0, tm, issue, 0)

    def drain(r, c):
        row_copy(r, 0).wait()
        row_copy(r, 1).wait()
        return c

    lax.fori_loop(0, tm, drain, 0)
    rt = rt_ref[0]
    y = rt[:, 4:5] * buf_ref[0] + rt[:, 5:6] * buf_ref[1]
    gate = mod_ref[0][:, 2 * D_MODEL:3 * D_MODEL]
    o_ref[0] = _residual_ln(x_ref[0], y, gate, g_ref[...], b_ref[...])


def _combine(slots, ys, route, x, mod, ln_g, ln_b, *, tm):
    nb, t, _ = x.shape
    row = lambda b, i, s: (b, i, 0)
    const2 = lambda b, i, s: (0, 0)
    return pl.pallas_call(
        functools.partial(_combine_kernel, tm=tm),
        grid_spec=pltpu.PrefetchScalarGridSpec(
            num_scalar_prefetch=1,
            grid=(nb, t // tm),
            in_specs=[
                pl.BlockSpec(memory_space=pl.ANY),
                pl.BlockSpec((1, tm, LANES), row),
                pl.BlockSpec((1, tm, D_MODEL), row),
                pl.BlockSpec((1, 1, 3 * D_MODEL), lambda b, i, s: (b, 0, 0)),
                pl.BlockSpec((1, D_MODEL), const2), pl.BlockSpec((1, D_MODEL), const2),
            ],
            out_specs=pl.BlockSpec((1, tm, D_MODEL), row),
            scratch_shapes=[pltpu.VMEM((2, tm, D_MODEL), F32), pltpu.SemaphoreType.DMA(())],
        ),
        out_shape=jax.ShapeDtypeStruct((nb, t, D_MODEL), F32),
        compiler_params=_cparams(("arbitrary", "arbitrary")),
        name="moe_combine",
    )(slots, ys, route, x, mod, ln_g.reshape(1, -1), ln_b.reshape(1, -1))


def _moe_plan(route, counts, *, tg, n_tiles):
    cnt = counts[0, :N_EXPERTS].astype(jnp.int32)
    tiles = (cnt + tg - 1) // tg
    tile_end = jnp.cumsum(tiles)
    start = (tile_end - tiles) * tg
    r = route.reshape(-1, LANES)
    e = r[:, 0:2].astype(jnp.int32)
    slots = (start[e] + r[:, 2:4].astype(jnp.int32)).reshape(-1)
    n_active = tile_end[-1:]
    j = jnp.minimum(jnp.arange(n_tiles, dtype=jnp.int32), n_active[0] - 1)
    tile_expert = jnp.sum((j[:, None] >= tile_end[None, :]).astype(jnp.int32), axis=1)
    return slots, tile_expert, n_active.astype(jnp.int32)


def _w_in_padded(w_in):
    z, xs, bm, cm = (0, 512), (512, 1024), (1024, 1152), (1152, 1280)
    dt, ql, kvl, kpe, gu, gv = (1280, 1288), (1288, 1544), (1544, 1672), (1672, 1704), (1704, 1960), (1960, 2216)
    seg = lambda ab: w_in[:, ab[0]:ab[1]]
    k_pe = seg(kpe)
    k_sw = jnp.concatenate([k_pe[:, MLA_ROPE // 2:], k_pe[:, :MLA_ROPE // 2]], axis=1)
    pad = jnp.zeros((D_MODEL, LANES - 2 * MLA_ROPE - SSD_HEADS), w_in.dtype)
    cols = [seg(z), seg(xs), seg(bm), seg(cm), seg(ql), seg(kvl), seg(gu), seg(gv), k_pe, k_sw, seg(dt), pad]
    return jnp.concatenate(cols, axis=1).astype(BF16)


def _forward(x, c, ln0_g, ln0_b, ada_w, ada_b, post_ln_g, post_ln_b, w_in, ssd_conv_w, ssd_conv_b,
             ssd_dt_bias, ssd_a_log, ssd_d, ssd_norm_w, mla_q_norm, mla_w_qb, mla_kv_norm, mla_w_kvb,
             gm_ln_g, gm_ln_b, gm_w_s, gm_b_s, w_out, ffn_w1, ffn_w3, ffn_w2,
             moe_router, moe_w1, moe_w3, moe_w2, *, tm, tq, tg, tc):
    nb, t, _ = x.shape
    mods = _modulations(c, ada_w, ada_b).reshape(2 * DEPTH, nb, 1, 3 * D_MODEL)
    tables = _rope_tables(t)
    for layer in range(DEPTH):
        mod_mix, mod_ffn = mods[2 * layer], mods[2 * layer + 1]
        w_pad = _w_in_padded(w_in[layer])
        if layer == 0:
            x, p_ssd, p_mla, p_gm, p_misc = _inproj(x, mod_mix, w_pad, ln0_g, ln0_b, tm=tm)
        else:
            p_ssd, p_mla, p_gm, p_misc = _inproj(x, mod_mix, w_pad, tm=tm)
        y_ssd = _ssd(p_ssd, p_misc, ssd_conv_w[layer], ssd_conv_b[layer], ssd_dt_bias[layer],
                     ssd_a_log[layer], ssd_d[layer], ssd_norm_w[layer])
        q, k, v = _mla_prep(p_mla, p_misc, mla_q_norm[layer], mla_kv_norm[layer], mla_w_qb[layer],
                            mla_w_kvb[layer], tables, tt=tm)
        y_mla = _flash(q, k, v, tq=tq)
        y_gm = _gmlp(p_gm, gm_ln_g[layer], gm_ln_b[layer], gm_w_s[layer], gm_b_s[layer])
        i = layer // 2
        if layer % 2 == 0:
            x, h = _outproj(y_ssd, y_mla, y_gm, x, mod_mix, mod_ffn, w_out[layer],
                            post_ln_g[layer, 0], post_ln_b[layer, 0], tm=tm)
            x = _ffn(h, x, mod_ffn, ffn_w1[i], ffn_w3[i], ffn_w2[i], post_ln_g[layer, 1], post_ln_b[layer, 1], tm=tm)
        else:
            x, h, route, counts = _outproj(y_ssd, y_mla, y_gm, x, mod_mix, mod_ffn, w_out[layer],
                                           post_ln_g[layer, 0], post_ln_b[layer, 0], moe_router[i], tm=tm)
            n_tiles = (2 * nb * t) // tg + N_EXPERTS
            slots, tile_expert, n_active = _moe_plan(route, counts, tg=tg, n_tiles=n_tiles)
            hs = _dispatch(slots, h.reshape(nb * t, D_MODEL), n_tiles * tg, tm=tc)
            ys = _moe_ffn(tile_expert, n_active, hs, moe_w1[i], moe_w3[i], moe_w2[i], tg=tg)
            x = _combine(slots, ys, route, x, mod_ffn, post_ln_g[layer, 1], post_ln_b[layer, 1], tm=tc)
    return x


def kernel(x, c, ln0_g, ln0_b, ada_w, ada_b, post_ln_g, post_ln_b, w_in, ssd_conv_w, ssd_conv_b, ssd_dt_bias, ssd_a_log, ssd_d, ssd_norm_w, mla_q_norm, mla_w_qb, mla_kv_norm, mla_w_kvb, gm_ln_g, gm_ln_b, gm_w_s, gm_b_s, w_out, ffn_w1, ffn_w3, ffn_w2, moe_router, moe_w1, moe_w3, moe_w2):
    t = x.shape[1]
    tm = min(512, t)
    return _forward(x, c, ln0_g, ln0_b, ada_w, ada_b, post_ln_g, post_ln_b, w_in, ssd_conv_w, ssd_conv_b,
                    ssd_dt_bias, ssd_a_log, ssd_d, ssd_norm_w, mla_q_norm, mla_w_qb, mla_kv_norm, mla_w_kvb,
                    gm_ln_g, gm_ln_b, gm_w_s, gm_b_s, w_out, ffn_w1, ffn_w3, ffn_w2,
                    moe_router, moe_w1, moe_w3, moe_w2, tm=tm, tq=tm, tg=min(256, t), tc=min(256, t))
```

```python
import functools
import math

import numpy as np
import jax
import jax.numpy as jnp
from jax import lax
from jax.experimental import pallas as pl
from jax.experimental.pallas import tpu as pltpu

F32 = jnp.float32
BF16 = jnp.bfloat16

D_MODEL = 1024
DEPTH = 2
SSD_INNER = 512
SSD_HEAD_DIM = 64
SSD_HEADS = 8
SSD_GROUPS = 2
SSD_STATE = 64
SSD_CONV = 4
CHUNK = 128
SSD_CONV_DIM = 768
MLA_HEADS = 4
MLA_Q_RANK = 256
MLA_KV_RANK = 128
MLA_NOPE = 64
MLA_ROPE = 32
MLA_V = 64
MLA_SCALE = (MLA_NOPE + MLA_ROPE) ** -0.5
ROPE_THETA = 10000.0
GM_GROUPS = 4
GM_GROUP_DIM = 64
GM_WIDTH = 256
D_FF = 2816
N_EXPERTS = 8
DN_ALPHA = (2 * DEPTH) ** 0.25
LN_EPS = 1e-5
RMS_EPS = 1e-6

LANES = 128
FF_CHUNK = 256
VMEM_LIMIT = 56 * 1024 * 1024

MISC_KPE = 0
MISC_KPE_SW = 32
MISC_DT = 64
D_IN_PAD = 1280 + 384 + 512 + 128


def _cparams(sem):
    return pltpu.CompilerParams(dimension_semantics=sem, vmem_limit_bytes=VMEM_LIMIT)


def _silu(x):
    return x * jax.nn.sigmoid(x)


def _gelu(x):
    return 0.5 * x * (1.0 + lax.erf(x * (2.0 ** -0.5)))


def _softplus(x):
    return jnp.maximum(x, 0.0) + jnp.log1p(jnp.exp(-jnp.abs(x)))


def _layer_norm(x, g, b):
    mu = jnp.mean(x, axis=-1, keepdims=True)
    xc = x - mu
    var = jnp.mean(xc * xc, axis=-1, keepdims=True)
    return xc * lax.rsqrt(var + LN_EPS) * g + b


def _rms(x):
    return x * lax.rsqrt(jnp.mean(x * x, axis=-1, keepdims=True) + RMS_EPS)


def _dot(a, b):
    return jnp.dot(a, b, preferred_element_type=F32)


def _dot_nt(a, b):
    return lax.dot_general(a, b, (((1,), (1,)), ((), ())), preferred_element_type=F32)


def _dot_tn(a, b):
    return lax.dot_general(a, b, (((0,), (0,)), ((), ())), preferred_element_type=F32)


def _mod_kernel(c_ref, w_ref, b_ref, o_ref):
    s = _silu(c_ref[...]).astype(BF16)
    o_ref[0] = _dot(s, w_ref[0].astype(BF16)) + b_ref[0]


def _modulations(c, ada_w, ada_b):
    nb = c.shape[0]
    n = ada_w.shape[0] * ada_w.shape[1]
    w = ada_w.reshape(n, D_MODEL, 3 * D_MODEL)
    b = ada_b.reshape(n, 1, 3 * D_MODEL)
    tn = 1024
    return pl.pallas_call(
        _mod_kernel,
        grid=(n, 3 * D_MODEL // tn),
        in_specs=[
            pl.BlockSpec((nb, D_MODEL), lambda j, k: (0, 0)),
            pl.BlockSpec((1, D_MODEL, tn), lambda j, k: (j, 0, k)),
            pl.BlockSpec((1, 1, tn), lambda j, k: (j, 0, k)),
        ],
        out_specs=pl.BlockSpec((1, nb, tn), lambda j, k: (j, 0, k)),
        out_shape=jax.ShapeDtypeStruct((n, nb, 3 * D_MODEL), F32),
        compiler_params=_cparams(("arbitrary", "arbitrary")),
        name="adaln_mod",
    )(c, w, b)


def _inproj_kernel(*refs, do_ln):
    if do_ln:
        x_ref, mod_ref, g_ref, b_ref, w_ref, xo_ref, ssd_ref, mla_ref, gm_ref, misc_ref = refs
    else:
        x_ref, mod_ref, w_ref, ssd_ref, mla_ref, gm_ref, misc_ref = refs
    x = x_ref[0]
    if do_ln:
        x = _layer_norm(x, g_ref[...], b_ref[...])
        xo_ref[0] = x
    m = mod_ref[0]
    h = x * (1.0 + m[:, D_MODEL:2 * D_MODEL]) + m[:, 0:D_MODEL]
    p = _dot(h.astype(BF16), w_ref[...])
    ssd_ref[0] = p[:, 0:1280].astype(BF16)
    mla_ref[0] = p[:, 1280:1664].astype(BF16)
    gm_ref[0] = p[:, 1664:2176].astype(BF16)
    misc_ref[0] = p[:, 2176:2304]


def _inproj(x, mod, w_pad, ln_g=None, ln_b=None, *, tm):
    nb, t, _ = x.shape
    do_ln = ln_g is not None
    row = lambda b, i: (b, i, 0)
    const2 = lambda b, i: (0, 0)
    in_specs = [pl.BlockSpec((1, tm, D_MODEL), row), pl.BlockSpec((1, 1, 3 * D_MODEL), lambda b, i: (b, 0, 0))]
    args = [x, mod]
    if do_ln:
        in_specs += [pl.BlockSpec((1, D_MODEL), const2), pl.BlockSpec((1, D_MODEL), const2)]
        args += [ln_g.reshape(1, D_MODEL), ln_b.reshape(1, D_MODEL)]
    in_specs.append(pl.BlockSpec((D_MODEL, D_IN_PAD), const2))
    args.append(w_pad)
    out_specs = [pl.BlockSpec((1, tm, 1280), row), pl.BlockSpec((1, tm, 384), row),
                 pl.BlockSpec((1, tm, 512), row), pl.BlockSpec((1, tm, LANES), row)]
    out_shape = [jax.ShapeDtypeStruct((nb, t, 1280), BF16), jax.ShapeDtypeStruct((nb, t, 384), BF16),
                 jax.ShapeDtypeStruct((nb, t, 512), BF16), jax.ShapeDtypeStruct((nb, t, LANES), F32)]
    if do_ln:
        out_specs = [pl.BlockSpec((1, tm, D_MODEL), row)] + out_specs
        out_shape = [jax.ShapeDtypeStruct((nb, t, D_MODEL), F32)] + out_shape
    return pl.pallas_call(
        functools.partial(_inproj_kernel, do_ln=do_ln),
        grid=(nb, t // tm),
        in_specs=in_specs, out_specs=out_specs, out_shape=out_shape,
        compiler_params=_cparams(("arbitrary", "arbitrary")),
        name="inproj_ln" if do_ln else "inproj",
    )(*args)


def _ssd_kernel(ssd_ref, misc_ref, cw_ref, cb_ref, dtb_ref, arow_ref, drow_ref, nw_ref, o_ref,
                xpad_ref, st_ref, y_ref, *, ts):
    L = CHUNK

    @pl.when(pl.program_id(1) == 0)
    def _():
        xpad_ref[0:8, :] = jnp.zeros((8, SSD_CONV_DIM), F32)
        st_ref[...] = jnp.zeros(st_ref.shape, F32)

    blk = ssd_ref[0]
    xpad_ref[8:8 + ts, :] = blk[:, SSD_INNER:SSD_INNER + SSD_CONV_DIM].astype(F32)
    cw = cw_ref[...]
    acc = cb_ref[...] + cw[3:4, :] * xpad_ref[8:8 + ts, :]
    for k in range(SSD_CONV - 1):
        acc = acc + cw[k:k + 1, :] * xpad_ref[pl.ds(5 + k, ts), :]
    xpad_ref[0:8, :] = xpad_ref[ts:ts + 8, :]
    xbc = _silu(acc)

    dtv_all = _softplus(misc_ref[0] + dtb_ref[...])
    a_all = dtv_all * arow_ref[...]
    row = lax.broadcasted_iota(jnp.int32, (L, LANES), 0)
    col = lax.broadcasted_iota(jnp.int32, (L, LANES), 1)
    causal = row >= col
    lo_half = col < SSD_HEAD_DIM
    lo_half_row = lo_half[0:1, :]
    heads_per_group = SSD_HEADS // SSD_GROUPS

    for c in range(ts // L):
        r0 = c * L
        xs = xbc[r0:r0 + L, 0:SSD_INNER]
        bm = xbc[r0:r0 + L, SSD_INNER:SSD_INNER + 128]
        cm = xbc[r0:r0 + L, SSD_INNER + 128:SSD_INNER + 256]
        dtv = dtv_all[r0:r0 + L, :]
        acs = a_all[r0:r0 + L, :]
        s = 1
        while s < L:
            acs = acs + jnp.where(row >= s, pltpu.roll(acs, s, 0), 0.0)
            s *= 2
        acs_t = acs.T
        atot = acs[L - 1:L, :]
        e_out = jnp.exp(acs)
        e_end = jnp.exp(atot - acs)
        e_tot = jnp.exp(atot)

        for g in range(SSD_GROUPS):
            in_group = (col >= SSD_STATE * g) & (col < SSD_STATE * (g + 1))
            bg = jnp.where(in_group, bm, 0.0).astype(BF16)
            cg = jnp.where(in_group, cm, 0.0).astype(BF16)
            cb = _dot_nt(cg, bg)
            for q in range(heads_per_group // 2):
                pair = g * (heads_per_group // 2) + q
                l0 = MISC_DT + 2 * pair
                l1 = l0 + 1
                lanes = slice(LANES * pair, LANES * pair + LANES)

                def pair_cols(arr):
                    return jnp.where(lo_half, arr[:, l0:l0 + 1], arr[:, l1:l1 + 1])

                x_p = xs[:, lanes]
                xdt = x_p * pair_cols(dtv)
                xdt_b = xdt.astype(BF16)
                halves = []
                for ll in (l0, l1):
                    decay = jnp.exp(jnp.where(causal, acs[:, ll:ll + 1] - acs_t[ll:ll + 1, :], -jnp.inf))
                    halves.append(_dot((cb * decay).astype(BF16), xdt_b))
                y = jnp.where(lo_half, halves[0], halves[1])
                st = st_ref[pair]
                y = y + pair_cols(e_out) * _dot(cg, st.astype(BF16)) + x_p * drow_ref[:, lanes]
                wst = (xdt * pair_cols(e_end)).astype(BF16)
                tot = jnp.where(lo_half_row, e_tot[:, l0:l0 + 1], e_tot[:, l1:l1 + 1])
                st_ref[pair] = tot * st + _dot_tn(bg, wst)
                y_ref[r0:r0 + L, lanes] = y

    gy = y_ref[...] * _silu(blk[:, 0:SSD_INNER].astype(F32))
    half = SSD_INNER // SSD_GROUPS
    for g in range(SSD_GROUPS):
        seg = gy[:, half * g:half * g + half]
        o_ref[0, :, half * g:half * g + half] = (_rms(seg) * nw_ref[:, half * g:half * g + half]).astype(BF16)


def _ssd(ssd_in, misc, conv_w, conv_b, dt_bias, a_log, d_skip, norm_w, *, ts):
    nb, t, _ = ssd_in.shape
    lane_pad = (MISC_DT, LANES - MISC_DT - SSD_HEADS)
    dtb = jnp.pad(dt_bias.astype(F32), lane_pad).reshape(1, LANES)
    arow = jnp.pad(-jnp.exp(a_log.astype(F32)), lane_pad).reshape(1, LANES)
    drow = jnp.repeat(d_skip.astype(F32), SSD_HEAD_DIM).reshape(1, SSD_INNER)
    row = lambda b, i: (b, i, 0)
    const2 = lambda b, i: (0, 0)
    return pl.pallas_call(
        functools.partial(_ssd_kernel, ts=ts),
        grid=(nb, t // ts),
        in_specs=[
            pl.BlockSpec((1, ts, 1280), row),
            pl.BlockSpec((1, ts, LANES), row),
            pl.BlockSpec((SSD_CONV, SSD_CONV_DIM), const2),
            pl.BlockSpec((1, SSD_CONV_DIM), const2),
            pl.BlockSpec((1, LANES), const2),
            pl.BlockSpec((1, LANES), const2),
            pl.BlockSpec((1, SSD_INNER), const2),
            pl.BlockSpec((1, SSD_INNER), const2),
        ],
        out_specs=pl.BlockSpec((1, ts, SSD_INNER), row),
        out_shape=jax.ShapeDtypeStruct((nb, t, SSD_INNER), BF16),
        scratch_shapes=[
            pltpu.VMEM((ts + 8, SSD_CONV_DIM), F32),
            pltpu.VMEM((SSD_HEADS // 2, SSD_GROUPS * SSD_STATE, 2 * SSD_HEAD_DIM), F32),
            pltpu.VMEM((ts, SSD_INNER), F32),
        ],
        compiler_params=_cparams(("arbitrary", "arbitrary")),
        name="ssd_scan",
    )(ssd_in, misc, conv_w, conv_b.reshape(1, -1), dtb, arow, drow, norm_w.reshape(1, -1))


def _gmlp_kernel(gm_ref, lg_ref, lb_ref, ws_ref, bs_ref, o_ref, *, ts):
    blk = gm_ref[0].astype(F32)
    gu = _gelu(blk[:, 0:GM_WIDTH])
    gv = _layer_norm(_gelu(blk[:, GM_WIDTH:2 * GM_WIDTH]), lg_ref[...], lb_ref[...]).astype(BF16)
    row = lax.broadcasted_iota(jnp.int32, (CHUNK, CHUNK), 0)
    col = lax.broadcasted_iota(jnp.int32, (CHUNK, CHUNK), 1)
    ws = [jnp.where(row >= col, ws_ref[g], 0.0).astype(BF16) for g in range(GM_GROUPS)]
    group = lax.broadcasted_iota(jnp.int32, (CHUNK, GM_WIDTH), 1) // GM_GROUP_DIM
    for c in range(ts // CHUNK):
        r0 = c * CHUNK
        v = gv[r0:r0 + CHUNK, :]
        s = _dot(ws[GM_GROUPS - 1], v)
        for g in range(GM_GROUPS - 2, -1, -1):
            s = jnp.where(group == g, _dot(ws[g], v), s)
        o_ref[0, r0:r0 + CHUNK, :] = (gu[r0:r0 + CHUNK, :] * (s + bs_ref[...])).astype(BF16)


def _gmlp(gm_in, ln_g, ln_b, w_s, b_s, *, ts):
    nb, t, _ = gm_in.shape
    bs = jnp.repeat(jnp.transpose(b_s), GM_GROUP_DIM, axis=1)
    row = lambda b, i: (b, i, 0)
    return pl.pallas_call(
        functools.partial(_gmlp_kernel, ts=ts),
        grid=(nb, t // ts),
        in_specs=[
            pl.BlockSpec((1, ts, 2 * GM_WIDTH), row),
            pl.BlockSpec((1, GM_WIDTH), lambda b, i: (0, 0)),
            pl.BlockSpec((1, GM_WIDTH), lambda b, i: (0, 0)),
            pl.BlockSpec((GM_GROUPS, CHUNK, CHUNK), lambda b, i: (0, 0, 0)),
            pl.BlockSpec((CHUNK, GM_WIDTH), lambda b, i: (0, 0)),
        ],
        out_specs=pl.BlockSpec((1, ts, GM_WIDTH), row),
        out_shape=jax.ShapeDtypeStruct((nb, t, GM_WIDTH), BF16),
        compiler_params=_cparams(("arbitrary", "arbitrary")),
        name="gmlp_gate",
    )(gm_in, ln_g.reshape(1, -1), ln_b.reshape(1, -1), w_s, bs)


def _mla_prep_kernel(mla_ref, misc_ref, qn_ref, kvn_ref, wqa_ref, wqb_ref, wk_ref, wv_ref,
                     cq_ref, sq_ref, ck_ref, sk_ref, q_ref, k_ref, v_ref):
    m = mla_ref[0].astype(F32)
    qn = (_rms(m[:, 0:MLA_Q_RANK]) * qn_ref[...]).astype(BF16)
    kvn = (_rms(m[:, MLA_Q_RANK:MLA_Q_RANK + MLA_KV_RANK]) * kvn_ref[...]).astype(BF16)
    qa = _dot(qn, wqa_ref[...])
    qb = _dot(qn, wqb_ref[...])
    kk = _dot(kvn, wk_ref[...])
    vv = _dot(kvn, wv_ref[...])
    misc = misc_ref[0]
    kr = pltpu.roll(misc, 64 - MISC_KPE, 1) * ck_ref[...] + pltpu.roll(misc, 64 - MISC_KPE_SW, 1) * sk_ref[...]
    cq = cq_ref[...]
    sq = sq_ref[...]
    ones_lane = lax.broadcasted_iota(jnp.int32, (vv.shape[0], LANES), 1) == MLA_V
    for h in range(MLA_HEADS):
        lo = LANES * h
        q_ref[0, h] = (qa[:, lo:lo + LANES] * cq + qb[:, lo:lo + LANES] * sq).astype(BF16)
        k_ref[0, h] = (kk[:, lo:lo + LANES] + kr).astype(BF16)
        v_ref[0, h] = jnp.where(ones_lane, 1.0, vv[:, lo:lo + LANES]).astype(BF16)


def _rope_tables(t):
    inv = ROPE_THETA ** (-jnp.arange(0, MLA_ROPE, 2, dtype=F32) / MLA_ROPE)
    ang = jnp.arange(t, dtype=F32)[:, None] * inv[None, :]
    cos, sin = jnp.cos(ang), jnp.sin(ang)
    z64 = jnp.zeros((t, 64), F32)
    z32 = jnp.zeros((t, 32), F32)
    one64 = jnp.ones((t, 64), F32)
    qscale = MLA_SCALE * math.log2(math.e)
    cq = jnp.concatenate([one64, cos, cos, z32], axis=1) * qscale
    sq = jnp.concatenate([z64, -sin, sin, z32], axis=1) * qscale
    ck = jnp.concatenate([z64, cos, cos, z32], axis=1)
    sk = jnp.concatenate([z64, -sin, sin, z32], axis=1)
    return cq, sq, ck, sk


def _mla_weights(w_qb, w_kvb):
    dq = MLA_NOPE + MLA_ROPE
    wq = w_qb.reshape(MLA_Q_RANK, MLA_HEADS, dq)
    nope, pe = wq[..., :MLA_NOPE], wq[..., MLA_NOPE:]
    pe_sw = jnp.concatenate([pe[..., MLA_ROPE // 2:], pe[..., :MLA_ROPE // 2]], axis=-1)
    zpad = jnp.zeros((MLA_Q_RANK, MLA_HEADS, LANES - dq), w_qb.dtype)
    wqa = jnp.concatenate([nope, pe, zpad], axis=-1).reshape(MLA_Q_RANK, MLA_HEADS * LANES)
    wqb = jnp.concatenate([jnp.zeros_like(nope), pe_sw, zpad], axis=-1).reshape(MLA_Q_RANK, MLA_HEADS * LANES)
    wkv = w_kvb.reshape(MLA_KV_RANK, MLA_HEADS, MLA_NOPE + MLA_V)
    wk = jnp.concatenate([wkv[..., :MLA_NOPE], jnp.zeros((MLA_KV_RANK, MLA_HEADS, LANES - MLA_NOPE), w_kvb.dtype)],
                         axis=-1).reshape(MLA_KV_RANK, MLA_HEADS * LANES)
    wv = jnp.concatenate([wkv[..., MLA_NOPE:], jnp.zeros((MLA_KV_RANK, MLA_HEADS, LANES - MLA_V), w_kvb.dtype)],
                         axis=-1).reshape(MLA_KV_RANK, MLA_HEADS * LANES)
    return wqa.astype(BF16), wqb.astype(BF16), wk.astype(BF16), wv.astype(BF16)


def _mla_prep(mla_in, misc, q_norm, kv_norm, w_qb, w_kvb, tables, *, tt):
    nb, t, _ = mla_in.shape
    wqa, wqb, wk, wv = _mla_weights(w_qb, w_kvb)
    row = lambda b, i: (b, i, 0)
    const2 = lambda b, i: (0, 0)
    tab = pl.BlockSpec((tt, LANES), lambda b, i: (i, 0))
    hrow = lambda b, i: (b, 0, i, 0)
    return pl.pallas_call(
        _mla_prep_kernel,
        grid=(nb, t // tt),
        in_specs=[
            pl.BlockSpec((1, tt, 384), row),
            pl.BlockSpec((1, tt, LANES), row),
            pl.BlockSpec((1, MLA_Q_RANK), const2),
            pl.BlockSpec((1, MLA_KV_RANK), const2),
            pl.BlockSpec((MLA_Q_RANK, MLA_HEADS * LANES), const2),
            pl.BlockSpec((MLA_Q_RANK, MLA_HEADS * LANES), const2),
            pl.BlockSpec((MLA_KV_RANK, MLA_HEADS * LANES), const2),
            pl.BlockSpec((MLA_KV_RANK, MLA_HEADS * LANES), const2),
            tab, tab, tab, tab,
        ],
        out_specs=[
            pl.BlockSpec((1, MLA_HEADS, tt, LANES), hrow),
            pl.BlockSpec((1, MLA_HEADS, tt, LANES), hrow),
            pl.BlockSpec((1, MLA_HEADS, tt, LANES), hrow),
        ],
        out_shape=[
            jax.ShapeDtypeStruct((nb, MLA_HEADS, t, LANES), BF16),
            jax.ShapeDtypeStruct((nb, MLA_HEADS, t, LANES), BF16),
            jax.ShapeDtypeStruct((nb, MLA_HEADS, t, LANES), BF16),
        ],
        compiler_params=_cparams(("arbitrary", "arbitrary")),
        name="mla_prep",
    )(mla_in, misc, q_norm.reshape(1, -1), kv_norm.reshape(1, -1), wqa, wqb, wk, wv, *tables)


def _flash_kernel(q_ref, k_ref, v_ref, o_ref, *, tq):
    i = pl.program_id(1)
    row = lax.broadcasted_iota(jnp.int32, (tq, tq), 0)
    col = lax.broadcasted_iota(jnp.int32, (tq, tq), 1)
    def step(j, carry, masked):
        start = pl.multiple_of(j * tq, tq)
        out = []
        for h in range(MLA_HEADS):
            m, acc = carry[h]
            kb = k_ref[0, h, pl.ds(start, tq), :]
            vb = v_ref[0, h, pl.ds(start, tq), :]
            s = _dot_nt(q_ref[0, h], kb)
            if masked:
                s = jnp.where(col <= row, s, -jnp.inf)
            m_new = jnp.maximum(m, jnp.max(s, axis=1, keepdims=True))
            p = jnp.exp2(s - m_new)
            acc = jnp.exp2(m - m_new) * acc + _dot(p.astype(BF16), vb)
            out.append((m_new, acc))
        return tuple(out)

    init = tuple((jnp.full((tq, 1), -jnp.inf, F32), jnp.zeros((tq, LANES), F32)) for _ in range(MLA_HEADS))
    carry = lax.fori_loop(0, i, functools.partial(step, masked=False), init)
    carry = step(i, carry, True)
    outs = [acc[:, 0:MLA_V] / acc[:, MLA_V:MLA_V + 1] for _, acc in carry]
    o_ref[0] = jnp.concatenate(outs, axis=-1).astype(BF16)


def _flash(q, k, v, *, tq):
    nb, nh, t, _ = q.shape
    return pl.pallas_call(
        functools.partial(_flash_kernel, tq=tq),
        grid=(nb, t // tq),
        in_specs=[
            pl.BlockSpec((1, nh, tq, LANES), lambda b, i: (b, 0, i, 0)),
            pl.BlockSpec((1, nh, t, LANES), lambda b, i: (b, 0, 0, 0)),
            pl.BlockSpec((1, nh, t, LANES), lambda b, i: (b, 0, 0, 0)),
        ],
        out_specs=pl.BlockSpec((1, tq, nh * MLA_V), lambda b, i: (b, i, 0)),
        out_shape=jax.ShapeDtypeStruct((nb, t, nh * MLA_V), BF16),
        compiler_params=_cparams(("arbitrary", "arbitrary")),
        name="mla_flash",
    )(q, k, v)


def _residual_ln(x, y, gate, g, b):
    return _layer_norm(DN_ALPHA * x + (1.0 + gate) * y, g, b)


def _outproj_kernel(*refs, route, tm):
    if route:
        (ys_ref, ym_ref, yg_ref, x_ref, mod1_ref, mod2_ref, w_ref, g_ref, b_ref, wr_ref,
         xo_ref, h_ref, rt_ref, cnt_ref, run_ref) = refs
    else:
        ys_ref, ym_ref, yg_ref, x_ref, mod1_ref, mod2_ref, w_ref, g_ref, b_ref, xo_ref, h_ref = refs
    y = (_dot(ys_ref[0], w_ref[0:512, :]) + _dot(ym_ref[0], w_ref[512:768, :])
         + _dot(yg_ref[0], w_ref[768:1024, :]))
    gate = mod1_ref[0][:, 2 * D_MODEL:3 * D_MODEL]
    xn = _residual_ln(x_ref[0], y, gate, g_ref[...], b_ref[...])
    xo_ref[0] = xn
    m2 = mod2_ref[0]
    h = xn * (1.0 + m2[:, D_MODEL:2 * D_MODEL]) + m2[:, 0:D_MODEL]
    if not route:
        h_ref[0] = h.astype(BF16)
        return
    h_ref[0] = h

    @pl.when((pl.program_id(0) == 0) & (pl.program_id(1) == 0))
    def _():
        run_ref[...] = jnp.zeros(run_ref.shape, F32)

    lane = lax.broadcasted_iota(jnp.int32, (tm, LANES), 1)
    logits = jnp.where(lane < N_EXPERTS, _dot(h.astype(BF16), wr_ref[...]), -jnp.inf)
    v1 = jnp.max(logits, axis=1, keepdims=True)
    i1 = jnp.min(jnp.where(logits == v1, lane, LANES), axis=1, keepdims=True)
    rest = jnp.where(lane == i1, -jnp.inf, logits)
    v2 = jnp.max(rest, axis=1, keepdims=True)
    i2 = jnp.min(jnp.where(rest == v2, lane, LANES), axis=1, keepdims=True)
    e2 = jnp.exp(v2 - v1)
    g1 = 1.0 / (1.0 + e2)
    g2 = e2 / (1.0 + e2)
    hot = ((lane == i1) | (lane == i2)).astype(F32)
    r_i = lax.broadcasted_iota(jnp.int32, (tm, tm), 0)
    c_i = lax.broadcasted_iota(jnp.int32, (tm, tm), 1)
    below = (c_i < r_i).astype(BF16)
    rank = _dot(below, hot.astype(BF16)) + run_ref[...]
    r1 = jnp.sum(jnp.where(lane == i1, rank, 0.0), axis=1, keepdims=True)
    r2 = jnp.sum(jnp.where(lane == i2, rank, 0.0), axis=1, keepdims=True)
    run = run_ref[...] + jnp.sum(hot, axis=0, keepdims=True)
    run_ref[...] = run
    cnt_ref[...] = jnp.broadcast_to(run, cnt_ref.shape)
    rt = jnp.where(lane == 0, i1.astype(F32), 0.0)
    rt = jnp.where(lane == 1, i2.astype(F32), rt)
    rt = jnp.where(lane == 2, r1, rt)
    rt = jnp.where(lane == 3, r2, rt)
    rt = jnp.where(lane == 4, g1, rt)
    rt = jnp.where(lane == 5, g2, rt)
    rt_ref[0] = rt


def _outproj(y_ssd, y_mla, y_gm, x, mod1, mod2, w_out, ln_g, ln_b, w_router=None, *, tm):
    nb, t, _ = x.shape
    route = w_router is not None
    row = lambda b, i: (b, i, 0)
    const2 = lambda b, i: (0, 0)
    modspec = pl.BlockSpec((1, 1, 3 * D_MODEL), lambda b, i: (b, 0, 0))
    in_specs = [
        pl.BlockSpec((1, tm, 512), row), pl.BlockSpec((1, tm, 256), row), pl.BlockSpec((1, tm, 256), row),
        pl.BlockSpec((1, tm, D_MODEL), row), modspec, modspec,
        pl.BlockSpec((D_MODEL, D_MODEL), const2),
        pl.BlockSpec((1, D_MODEL), const2), pl.BlockSpec((1, D_MODEL), const2),
    ]
    args = [y_ssd, y_mla, y_gm, x, mod1, mod2, w_out.astype(BF16), ln_g.reshape(1, -1), ln_b.reshape(1, -1)]
    out_specs = [pl.BlockSpec((1, tm, D_MODEL), row), pl.BlockSpec((1, tm, D_MODEL), row)]
    out_shape = [jax.ShapeDtypeStruct((nb, t, D_MODEL), F32),
                 jax.ShapeDtypeStruct((nb, t, D_MODEL), F32 if route else BF16)]
    scratch = []
    if route:
        wr = jnp.pad(w_router, ((0, 0), (0, LANES - N_EXPERTS))).astype(BF16)
        in_specs.append(pl.BlockSpec((D_MODEL, LANES), const2))
        args.append(wr)
        out_specs += [pl.BlockSpec((1, tm, LANES), row), pl.BlockSpec((8, LANES), const2)]
        out_shape += [jax.ShapeDtypeStruct((nb, t, LANES), F32), jax.ShapeDtypeStruct((8, LANES), F32)]
        scratch = [pltpu.VMEM((1, LANES), F32)]
    return pl.pallas_call(
        functools.partial(_outproj_kernel, route=route, tm=tm),
        grid=(nb, t // tm),
        in_specs=in_specs, out_specs=out_specs, out_shape=out_shape, scratch_shapes=scratch,
        compiler_params=_cparams(("arbitrary", "arbitrary")),
        name="outproj_route" if route else "outproj",
    )(*args)


def _swiglu_acc(h, w1_ref, w3_ref, w2_ref, widx, side_work=None):
    acc = None
    for j in range(D_FF // FF_CHUNK):
        lo = j * FF_CHUNK
        if side_work is not None:
            side_work(j)
        a = _dot(h, w1_ref[widx + (slice(None), slice(lo, lo + FF_CHUNK))])
        g = _dot(h, w3_ref[widx + (slice(None), slice(lo, lo + FF_CHUNK))])
        p = (_silu(a) * g).astype(BF16)
        part = _dot(p, w2_ref[widx + (slice(lo, lo + FF_CHUNK), slice(None))])
        acc = part if acc is None else acc + part
    return acc


def _ffn_kernel(h_ref, x_ref, mod_ref, w1_ref, w3_ref, w2_ref, g_ref, b_ref, o_ref):
    y = _swiglu_acc(h_ref[0], w1_ref, w3_ref, w2_ref, ())
    gate = mod_ref[0][:, 2 * D_MODEL:3 * D_MODEL]
    o_ref[0] = _residual_ln(x_ref[0], y, gate, g_ref[...], b_ref[...])


def _ffn(h, x, mod, w1, w3, w2, ln_g, ln_b, *, tm):
    nb, t, _ = x.shape
    row = lambda b, i: (b, i, 0)
    const2 = lambda b, i: (0, 0)
    wspec = lambda shape: pl.BlockSpec(shape, const2, pipeline_mode=pl.Buffered(1))
    return pl.pallas_call(
        _ffn_kernel,
        grid=(nb, t // tm),
        in_specs=[
            pl.BlockSpec((1, tm, D_MODEL), row), pl.BlockSpec((1, tm, D_MODEL), row),
            pl.BlockSpec((1, 1, 3 * D_MODEL), lambda b, i: (b, 0, 0)),
            wspec((D_MODEL, D_FF)), wspec((D_MODEL, D_FF)), wspec((D_FF, D_MODEL)),
            pl.BlockSpec((1, D_MODEL), const2), pl.BlockSpec((1, D_MODEL), const2),
        ],
        out_specs=pl.BlockSpec((1, tm, D_MODEL), row),
        out_shape=jax.ShapeDtypeStruct((nb, t, D_MODEL), F32),
        compiler_params=_cparams(("arbitrary", "arbitrary")),
        name="ffn_dense",
    )(h, x, mod, w1.astype(BF16), w3.astype(BF16), w2.astype(BF16), ln_g.reshape(1, -1), ln_b.reshape(1, -1))


def _invert_kernel(slot_ref, init_ref, inv_ref, sem, *, n_pairs):
    cp = pltpu.make_async_copy(init_ref, inv_ref, sem)
    cp.start()
    cp.wait()

    def body(i, c):
        inv_ref[slot_ref[i]] = i
        return c

    lax.fori_loop(0, n_pairs, body, 0, unroll=8)


def _moe_invert(slots, inv_init):
    return pl.pallas_call(
        functools.partial(_invert_kernel, n_pairs=slots.shape[0]),
        grid_spec=pltpu.PrefetchScalarGridSpec(
            num_scalar_prefetch=1,
            grid=(1,),
            in_specs=[pl.BlockSpec(memory_space=pl.ANY)],
            out_specs=pl.BlockSpec(memory_space=pltpu.SMEM),
            scratch_shapes=[pltpu.SemaphoreType.DMA(())],
        ),
        out_shape=jax.ShapeDtypeStruct(inv_init.shape, jnp.int32),
        compiler_params=_cparams(("arbitrary",)),
        name="moe_invert",
    )(slots, inv_init)


def _moe_ffn_kernel(te_ref, inv_ref, h_hbm, w1_ref, w3_ref, w2_ref, y_hbm, xbuf, obuf, gsem, ssem, *, tg, n_tok):
    del te_ref
    j = pl.program_id(0)
    last = pl.num_programs(0) - 1
    slot = lax.rem(j, 2)
    other = 1 - slot

    def gather_row(tile, r, sl):
        tok = jnp.minimum(inv_ref[(tile + 1) * tg + r] >> 1, n_tok - 1)
        return pltpu.make_async_copy(h_hbm.at[pl.ds(tok, 1), :], xbuf.at[sl, pl.ds(r, 1), :], gsem.at[sl])

    def scatter_row(tile, r, sl):
        dst = inv_ref[(tile + 1) * tg + r]
        return pltpu.make_async_copy(obuf.at[sl, pl.ds(r, 1), :], y_hbm.at[pl.ds(dst, 1), :], ssem.at[sl])

    def wait_gather(sl):
        pltpu.make_async_copy(h_hbm.at[pl.ds(0, tg), :], xbuf.at[sl], gsem.at[sl]).wait()

    def wait_scatter(sl):
        pltpu.make_async_copy(obuf.at[sl], y_hbm.at[pl.ds(0, tg), :], ssem.at[sl]).wait()

    @pl.when(j == 0)
    def _():
        obuf[1] = jnp.zeros((tg, D_MODEL), F32)

        def first(r, c):
            gather_row(0, r, 0).start()
            return c

        lax.fori_loop(0, tg, first, 0)

    @pl.when(j > 0)
    def _():
        wait_scatter(slot)

    wait_gather(slot)
    x = xbuf[slot].astype(BF16)
    nxt = jnp.minimum(j + 1, last)
    n_chunks = D_FF // FF_CHUNK
    per = -(-tg // n_chunks)

    def side_work(c):
        for r in range(c * per, min((c + 1) * per, tg)):
            gather_row(nxt, r, other).start()
            scatter_row(j - 1, r, other).start()

    obuf[slot] = _swiglu_acc(x, w1_ref, w3_ref, w2_ref, (0,), side_work)

    @pl.when(j == last)
    def _():
        wait_gather(other)
        wait_scatter(other)

        def final(r, c):
            scatter_row(j, r, slot).start()
            return c

        lax.fori_loop(0, tg, final, 0)
        wait_scatter(slot)


def _moe_ffn(tile_expert, inv, h, w1, w3, w2, *, tg, n_out):
    n_tiles = tile_expert.shape[0]
    wmap = lambda j, te, iv: (te[j], 0, 0)
    return pl.pallas_call(
        functools.partial(_moe_ffn_kernel, tg=tg, n_tok=h.shape[0]),
        grid_spec=pltpu.PrefetchScalarGridSpec(
            num_scalar_prefetch=2,
            grid=(n_tiles,),
            in_specs=[
                pl.BlockSpec(memory_space=pl.ANY),
                pl.BlockSpec((1, D_MODEL, D_FF), wmap),
                pl.BlockSpec((1, D_MODEL, D_FF), wmap),
                pl.BlockSpec((1, D_FF, D_MODEL), wmap),
            ],
            out_specs=pl.BlockSpec(memory_space=pl.ANY),
            scratch_shapes=[
                pltpu.VMEM((2, tg, D_MODEL), F32), pltpu.VMEM((2, tg, D_MODEL), F32),
                pltpu.SemaphoreType.DMA((2,)), pltpu.SemaphoreType.DMA((2,)),
            ],
        ),
        out_shape=jax.ShapeDtypeStruct((n_out, D_MODEL), F32),
        compiler_params=_cparams(("arbitrary",)),
        name="moe_ffn",
    )(tile_expert, inv, h, w1.astype(BF16), w3.astype(BF16), w2.astype(BF16))


def _combine_kernel(y_ref, rt_ref, x_ref, mod_ref, g_ref, b_ref, o_ref):
    yy = y_ref[...]
    rt = rt_ref[0]
    y = rt[:, 4:5] * yy[:, 0:D_MODEL] + rt[:, 5:6] * yy[:, D_MODEL:2 * D_MODEL]
    gate = mod_ref[0][:, 2 * D_MODEL:3 * D_MODEL]
    o_ref[0] = _residual_ln(x_ref[0], y, gate, g_ref[...], b_ref[...])


def _combine(y2, route, x, mod, ln_g, ln_b, *, tm):
    nb, t, _ = x.shape
    row = lambda b, i: (b, i, 0)
    const2 = lambda b, i: (0, 0)
    pairs = y2.reshape(y2.shape[0] // 2, 2 * D_MODEL)
    return pl.pallas_call(
        _combine_kernel,
        grid=(nb, t // tm),
        in_specs=[
            pl.BlockSpec((tm, 2 * D_MODEL), lambda b, i: (b * (t // tm) + i, 0)),
            pl.BlockSpec((1, tm, LANES), row),
            pl.BlockSpec((1, tm, D_MODEL), row),
            pl.BlockSpec((1, 1, 3 * D_MODEL), lambda b, i: (b, 0, 0)),
            pl.BlockSpec((1, D_MODEL), const2), pl.BlockSpec((1, D_MODEL), const2),
        ],
        out_specs=pl.BlockSpec((1, tm, D_MODEL), row),
        out_shape=jax.ShapeDtypeStruct((nb, t, D_MODEL), F32),
        compiler_params=_cparams(("arbitrary", "arbitrary")),
        name="moe_combine",
    )(pairs, route, x, mod, ln_g.reshape(1, -1), ln_b.reshape(1, -1))


def _moe_plan(route, counts, *, tg, n_tiles):
    n_pairs = 2 * route.shape[0] * route.shape[1]
    cnt = counts[0, :N_EXPERTS].astype(jnp.int32)
    tiles = (cnt + tg - 1) // tg
    tile_end = jnp.cumsum(tiles)
    start = (tile_end - tiles) * tg
    r = route.reshape(-1, LANES)
    e = r[:, 0:2].astype(jnp.int32)
    slots = (start[e] + r[:, 2:4].astype(jnp.int32)).reshape(-1) + tg
    n_active = tile_end[-1]
    j = jnp.minimum(jnp.arange(n_tiles, dtype=jnp.int32), n_active - 1)
    tile_expert = jnp.sum((j[:, None] >= tile_end[None, :]).astype(jnp.int32), axis=1)
    s = jnp.arange(n_tiles * tg, dtype=jnp.int32)
    es = jnp.minimum(jnp.sum((s[:, None] >= (tile_end * tg)[None, :]).astype(jnp.int32), axis=1), N_EXPERTS - 1)
    pad_before = jnp.cumsum(tiles * tg - cnt) - (tiles * tg - cnt)
    in_group = s - start[es]
    pad_id = jnp.where(s >= n_active * tg, s - n_pairs, pad_before[es] + in_group - cnt[es])
    n_spare = n_tiles * tg - n_pairs
    body = n_pairs + jnp.clip(pad_id, 0, n_spare - 1)
    head = n_pairs + n_spare + jnp.arange(tg, dtype=jnp.int32)
    return slots, tile_expert, jnp.concatenate([head, body]), n_pairs + n_spare + tg


def _w_in_padded(w_in):
    z, xs, bm, cm = (0, 512), (512, 1024), (1024, 1152), (1152, 1280)
    dt, ql, kvl, kpe, gu, gv = (1280, 1288), (1288, 1544), (1544, 1672), (1672, 1704), (1704, 1960), (1960, 2216)
    seg = lambda ab: w_in[:, ab[0]:ab[1]]
    k_pe = seg(kpe)
    k_sw = jnp.concatenate([k_pe[:, MLA_ROPE // 2:], k_pe[:, :MLA_ROPE // 2]], axis=1)
    pad = jnp.zeros((D_MODEL, LANES - 2 * MLA_ROPE - SSD_HEADS), w_in.dtype)
    cols = [seg(z), seg(xs), seg(bm), seg(cm), seg(ql), seg(kvl), seg(gu), seg(gv), k_pe, k_sw, seg(dt), pad]
    return jnp.concatenate(cols, axis=1).astype(BF16)


def _forward(x, c, ln0_g, ln0_b, ada_w, ada_b, post_ln_g, post_ln_b, w_in, ssd_conv_w, ssd_conv_b,
             ssd_dt_bias, ssd_a_log, ssd_d, ssd_norm_w, mla_q_norm, mla_w_qb, mla_kv_norm, mla_w_kvb,
             gm_ln_g, gm_ln_b, gm_w_s, gm_b_s, w_out, ffn_w1, ffn_w3, ffn_w2,
             moe_router, moe_w1, moe_w3, moe_w2, *, tm, tq, tg, tc):
    nb, t, _ = x.shape
    mods = _modulations(c, ada_w, ada_b).reshape(2 * DEPTH, nb, 1, 3 * D_MODEL)
    tables = _rope_tables(t)
    for layer in range(DEPTH):
        mod_mix, mod_ffn = mods[2 * layer], mods[2 * layer + 1]
        w_pad = _w_in_padded(w_in[layer])
        if layer == 0:
            x, p_ssd, p_mla, p_gm, p_misc = _inproj(x, mod_mix, w_pad, ln0_g, ln0_b, tm=tm)
        else:
            p_ssd, p_mla, p_gm, p_misc = _inproj(x, mod_mix, w_pad, tm=tm)
        y_ssd = _ssd(p_ssd, p_misc, ssd_conv_w[layer], ssd_conv_b[layer], ssd_dt_bias[layer],
                     ssd_a_log[layer], ssd_d[layer], ssd_norm_w[layer], ts=tm)
        q, k, v = _mla_prep(p_mla, p_misc, mla_q_norm[layer], mla_kv_norm[layer], mla_w_qb[layer],
                            mla_w_kvb[layer], tables, tt=tm)
        y_mla = _flash(q, k, v, tq=tq)
        y_gm = _gmlp(p_gm, gm_ln_g[layer], gm_ln_b[layer], gm_w_s[layer], gm_b_s[layer], ts=tm)
        i = layer // 2
        if layer % 2 == 0:
            x, h = _outproj(y_ssd, y_mla, y_gm, x, mod_mix, mod_ffn, w_out[layer],
                            post_ln_g[layer, 0], post_ln_b[layer, 0], tm=tm)
            x = _ffn(h, x, mod_ffn, ffn_w1[i], ffn_w3[i], ffn_w2[i], post_ln_g[layer, 1], post_ln_b[layer, 1], tm=tm)
        else:
            x, h, route, counts = _outproj(y_ssd, y_mla, y_gm, x, mod_mix, mod_ffn, w_out[layer],
                                           post_ln_g[layer, 0], post_ln_b[layer, 0], moe_router[i], tm=tm)
            n_tiles = (2 * nb * t) // tg + N_EXPERTS
            slots, tile_expert, inv_init, n_out = _moe_plan(route, counts, tg=tg, n_tiles=n_tiles)
            inv = _moe_invert(slots, inv_init)
            y2 = _moe_ffn(tile_expert, inv, h.reshape(nb * t, D_MODEL), moe_w1[i], moe_w3[i], moe_w2[i],
                          tg=tg, n_out=n_out)
            x = _combine(y2, route, x, mod_ffn, post_ln_g[layer, 1], post_ln_b[layer, 1], tm=tc)
    return x


def kernel(x, c, ln0_g, ln0_b, ada_w, ada_b, post_ln_g, post_ln_b, w_in, ssd_conv_w, ssd_conv_b, ssd_dt_bias, ssd_a_log, ssd_d, ssd_norm_w, mla_q_norm, mla_w_qb, mla_kv_norm, mla_w_kvb, gm_ln_g, gm_ln_b, gm_w_s, gm_b_s, w_out, ffn_w1, ffn_w3, ffn_w2, moe_router, moe_w1, moe_w3, moe_w2):
    t = x.shape[1]
    tm = min(512, t)
    return _forward(x, c, ln0_g, ln0_b, ada_w, ada_b, post_ln_g, post_ln_b, w_in, ssd_conv_w, ssd_conv_b,
                    ssd_dt_bias, ssd_a_log, ssd_d, ssd_norm_w, mla_q_norm, mla_w_qb, mla_kv_norm, mla_w_kvb,
                    gm_ln_g, gm_ln_b, gm_w_s, gm_b_s, w_out, ffn_w1, ffn_w3, ffn_w2,
                    moe_router, moe_w1, moe_w3, moe_w2, tm=tm, tq=tm, tg=tm, tc=tm)
```

```python
import functools
import math

import numpy as np
import jax
import jax.numpy as jnp
from jax import lax
from jax.experimental import pallas as pl
from jax.experimental.pallas import tpu as pltpu

F32 = jnp.float32
BF16 = jnp.bfloat16

D_MODEL = 1024
DEPTH = 2
SSD_INNER = 512
SSD_HEAD_DIM = 64
SSD_HEADS = 8
SSD_GROUPS = 2
SSD_STATE = 64
SSD_CONV = 4
CHUNK = 128
SSD_CONV_DIM = 768
MLA_HEADS = 4
MLA_Q_RANK = 256
MLA_KV_RANK = 128
MLA_NOPE = 64
MLA_ROPE = 32
MLA_V = 64
MLA_SCALE = (MLA_NOPE + MLA_ROPE) ** -0.5
ROPE_THETA = 10000.0
GM_GROUPS = 4
GM_GROUP_DIM = 64
GM_WIDTH = 256
D_FF = 2816
N_EXPERTS = 8
DN_ALPHA = (2 * DEPTH) ** 0.25
LN_EPS = 1e-5
RMS_EPS = 1e-6

LANES = 128
FF_CHUNK = 256
VMEM_LIMIT = 56 * 1024 * 1024

MISC_KPE = 0
MISC_KPE_SW = 32
MISC_DT = 64
D_IN_PAD = 1280 + 384 + 512 + 128


def _cparams(sem):
    return pltpu.CompilerParams(dimension_semantics=sem, vmem_limit_bytes=VMEM_LIMIT)


def _silu(x):
    return x * jax.nn.sigmoid(x)


def _gelu(x):
    return 0.5 * x * (1.0 + lax.erf(x * (2.0 ** -0.5)))


def _softplus(x):
    return jnp.maximum(x, 0.0) + jnp.log1p(jnp.exp(-jnp.abs(x)))


def _layer_norm(x, g, b):
    mu = jnp.mean(x, axis=-1, keepdims=True)
    xc = x - mu
    var = jnp.mean(xc * xc, axis=-1, keepdims=True)
    return xc * lax.rsqrt(var + LN_EPS) * g + b


def _rms(x):
    return x * lax.rsqrt(jnp.mean(x * x, axis=-1, keepdims=True) + RMS_EPS)


def _dot(a, b):
    return jnp.dot(a, b, preferred_element_type=F32)


def _dot_nt(a, b):
    return lax.dot_general(a, b, (((1,), (1,)), ((), ())), preferred_element_type=F32)


def _dot_tn(a, b):
    return lax.dot_general(a, b, (((0,), (0,)), ((), ())), preferred_element_type=F32)


def _mod_kernel(c_ref, w_ref, b_ref, o_ref):
    s = _silu(c_ref[...]).astype(BF16)
    o_ref[0] = _dot(s, w_ref[0].astype(BF16)) + b_ref[0]


def _modulations(c, ada_w, ada_b):
    nb = c.shape[0]
    n = ada_w.shape[0] * ada_w.shape[1]
    w = ada_w.reshape(n, D_MODEL, 3 * D_MODEL)
    b = ada_b.reshape(n, 1, 3 * D_MODEL)
    tn = 1024
    return pl.pallas_call(
        _mod_kernel,
        grid=(n, 3 * D_MODEL // tn),
        in_specs=[
            pl.BlockSpec((nb, D_MODEL), lambda j, k: (0, 0)),
            pl.BlockSpec((1, D_MODEL, tn), lambda j, k: (j, 0, k)),
            pl.BlockSpec((1, 1, tn), lambda j, k: (j, 0, k)),
        ],
        out_specs=pl.BlockSpec((1, nb, tn), lambda j, k: (j, 0, k)),
        out_shape=jax.ShapeDtypeStruct((n, nb, 3 * D_MODEL), F32),
        compiler_params=_cparams(("arbitrary", "arbitrary")),
        name="adaln_mod",
    )(c, w, b)


def _inproj_kernel(*refs, do_ln):
    if do_ln:
        x_ref, mod_ref, g_ref, b_ref, w_ref, xo_ref, ssd_ref, mla_ref, gm_ref, misc_ref = refs
    else:
        x_ref, mod_ref, w_ref, ssd_ref, mla_ref, gm_ref, misc_ref = refs
    x = x_ref[0]
    if do_ln:
        x = _layer_norm(x, g_ref[...], b_ref[...])
        xo_ref[0] = x
    m = mod_ref[0]
    h = x * (1.0 + m[:, D_MODEL:2 * D_MODEL]) + m[:, 0:D_MODEL]
    p = _dot(h.astype(BF16), w_ref[...])
    ssd_ref[0] = p[:, 0:1280].astype(BF16)
    mla_ref[0] = p[:, 1280:1664].astype(BF16)
    gm_ref[0] = p[:, 1664:2176].astype(BF16)
    misc_ref[0] = p[:, 2176:2304]


def _inproj(x, mod, w_pad, ln_g=None, ln_b=None, *, tm):
    nb, t, _ = x.shape
    do_ln = ln_g is not None
    row = lambda b, i: (b, i, 0)
    const2 = lambda b, i: (0, 0)
    in_specs = [pl.BlockSpec((1, tm, D_MODEL), row), pl.BlockSpec((1, 1, 3 * D_MODEL), lambda b, i: (b, 0, 0))]
    args = [x, mod]
    if do_ln:
        in_specs += [pl.BlockSpec((1, D_MODEL), const2), pl.BlockSpec((1, D_MODEL), const2)]
        args += [ln_g.reshape(1, D_MODEL), ln_b.reshape(1, D_MODEL)]
    in_specs.append(pl.BlockSpec((D_MODEL, D_IN_PAD), const2))
    args.append(w_pad)
    out_specs = [pl.BlockSpec((1, tm, 1280), row), pl.BlockSpec((1, tm, 384), row),
                 pl.BlockSpec((1, tm, 512), row), pl.BlockSpec((1, tm, LANES), row)]
    out_shape = [jax.ShapeDtypeStruct((nb, t, 1280), BF16), jax.ShapeDtypeStruct((nb, t, 384), BF16),
                 jax.ShapeDtypeStruct((nb, t, 512), BF16), jax.ShapeDtypeStruct((nb, t, LANES), F32)]
    if do_ln:
        out_specs = [pl.BlockSpec((1, tm, D_MODEL), row)] + out_specs
        out_shape = [jax.ShapeDtypeStruct((nb, t, D_MODEL), F32)] + out_shape
    return pl.pallas_call(
        functools.partial(_inproj_kernel, do_ln=do_ln),
        grid=(nb, t // tm),
        in_specs=in_specs, out_specs=out_specs, out_shape=out_shape,
        compiler_params=_cparams(("arbitrary", "arbitrary")),
        name="inproj_ln" if do_ln else "inproj",
    )(*args)


def _ssd_kernel(ssd_ref, misc_ref, cw_ref, cb_ref, dtb_ref, arow_ref, drow_ref, nw_ref, o_ref,
                xpad_ref, st_ref, y_ref, *, ts):
    L = CHUNK

    @pl.when(pl.program_id(1) == 0)
    def _():
        xpad_ref[0:8, :] = jnp.zeros((8, SSD_CONV_DIM), F32)
        st_ref[...] = jnp.zeros(st_ref.shape, F32)

    blk = ssd_ref[0]
    xpad_ref[8:8 + ts, :] = blk[:, SSD_INNER:SSD_INNER + SSD_CONV_DIM].astype(F32)
    cw = cw_ref[...]
    acc = cb_ref[...] + cw[3:4, :] * xpad_ref[8:8 + ts, :]
    for k in range(SSD_CONV - 1):
        acc = acc + cw[k:k + 1, :] * xpad_ref[pl.ds(5 + k, ts), :]
    xpad_ref[0:8, :] = xpad_ref[ts:ts + 8, :]
    xbc = _silu(acc)

    dtv_all = _softplus(misc_ref[0] + dtb_ref[...])
    a_all = dtv_all * arow_ref[...]
    row = lax.broadcasted_iota(jnp.int32, (L, LANES), 0)
    col = lax.broadcasted_iota(jnp.int32, (L, LANES), 1)
    causal = row >= col
    lo_half = col < SSD_HEAD_DIM
    lo_half_row = lo_half[0:1, :]
    heads_per_group = SSD_HEADS // SSD_GROUPS

    for c in range(ts // L):
        r0 = c * L
        xs = xbc[r0:r0 + L, 0:SSD_INNER]
        bm = xbc[r0:r0 + L, SSD_INNER:SSD_INNER + 128]
        cm = xbc[r0:r0 + L, SSD_INNER + 128:SSD_INNER + 256]
        dtv = dtv_all[r0:r0 + L, :]
        acs = a_all[r0:r0 + L, :]
        s = 1
        while s < L:
            acs = acs + jnp.where(row >= s, pltpu.roll(acs, s, 0), 0.0)
            s *= 2
        acs_t = acs.T
        atot = acs[L - 1:L, :]
        e_out = jnp.exp(acs)
        e_end = jnp.exp(atot - acs)
        e_tot = jnp.exp(atot)

        for g in range(SSD_GROUPS):
            in_group = (col >= SSD_STATE * g) & (col < SSD_STATE * (g + 1))
            bg = jnp.where(in_group, bm, 0.0).astype(BF16)
            cg = jnp.where(in_group, cm, 0.0).astype(BF16)
            cb = _dot_nt(cg, bg)
            for q in range(heads_per_group // 2):
                pair = g * (heads_per_group // 2) + q
                l0 = MISC_DT + 2 * pair
                l1 = l0 + 1
                lanes = slice(LANES * pair, LANES * pair + LANES)

                def pair_cols(arr):
                    return jnp.where(lo_half, arr[:, l0:l0 + 1], arr[:, l1:l1 + 1])

                x_p = xs[:, lanes]
                xdt = x_p * pair_cols(dtv)
                xdt_b = xdt.astype(BF16)
                halves = []
                for ll in (l0, l1):
                    decay = jnp.exp(jnp.where(causal, acs[:, ll:ll + 1] - acs_t[ll:ll + 1, :], -jnp.inf))
                    halves.append(_dot((cb * decay).astype(BF16), xdt_b))
                y = jnp.where(lo_half, halves[0], halves[1])
                st = st_ref[pair]
                y = y + pair_cols(e_out) * _dot(cg, st.astype(BF16)) + x_p * drow_ref[:, lanes]
                wst = (xdt * pair_cols(e_end)).astype(BF16)
                tot = jnp.where(lo_half_row, e_tot[:, l0:l0 + 1], e_tot[:, l1:l1 + 1])
                st_ref[pair] = tot * st + _dot_tn(bg, wst)
                y_ref[r0:r0 + L, lanes] = y

    gy = y_ref[...] * _silu(blk[:, 0:SSD_INNER].astype(F32))
    half = SSD_INNER // SSD_GROUPS
    for g in range(SSD_GROUPS):
        seg = gy[:, half * g:half * g + half]
        o_ref[0, :, half * g:half * g + half] = (_rms(seg) * nw_ref[:, half * g:half * g + half]).astype(BF16)


def _ssd(ssd_in, misc, conv_w, conv_b, dt_bias, a_log, d_skip, norm_w, *, ts):
    nb, t, _ = ssd_in.shape
    lane_pad = (MISC_DT, LANES - MISC_DT - SSD_HEADS)
    dtb = jnp.pad(dt_bias.astype(F32), lane_pad).reshape(1, LANES)
    arow = jnp.pad(-jnp.exp(a_log.astype(F32)), lane_pad).reshape(1, LANES)
    drow = jnp.repeat(d_skip.astype(F32), SSD_HEAD_DIM).reshape(1, SSD_INNER)
    row = lambda b, i: (b, i, 0)
    const2 = lambda b, i: (0, 0)
    return pl.pallas_call(
        functools.partial(_ssd_kernel, ts=ts),
        grid=(nb, t // ts),
        in_specs=[
            pl.BlockSpec((1, ts, 1280), row),
            pl.BlockSpec((1, ts, LANES), row),
            pl.BlockSpec((SSD_CONV, SSD_CONV_DIM), const2),
            pl.BlockSpec((1, SSD_CONV_DIM), const2),
            pl.BlockSpec((1, LANES), const2),
            pl.BlockSpec((1, LANES), const2),
            pl.BlockSpec((1, SSD_INNER), const2),
            pl.BlockSpec((1, SSD_INNER), const2),
        ],
        out_specs=pl.BlockSpec((1, ts, SSD_INNER), row),
        out_shape=jax.ShapeDtypeStruct((nb, t, SSD_INNER), BF16),
        scratch_shapes=[
            pltpu.VMEM((ts + 8, SSD_CONV_DIM), F32),
            pltpu.VMEM((SSD_HEADS // 2, SSD_GROUPS * SSD_STATE, 2 * SSD_HEAD_DIM), F32),
            pltpu.VMEM((ts, SSD_INNER), F32),
        ],
        compiler_params=_cparams(("arbitrary", "arbitrary")),
        name="ssd_scan",
    )(ssd_in, misc, conv_w, conv_b.reshape(1, -1), dtb, arow, drow, norm_w.reshape(1, -1))


def _gmlp_kernel(gm_ref, lg_ref, lb_ref, ws_ref, bs_ref, o_ref, *, ts):
    blk = gm_ref[0].astype(F32)
    gu = _gelu(blk[:, 0:GM_WIDTH])
    gv = _layer_norm(_gelu(blk[:, GM_WIDTH:2 * GM_WIDTH]), lg_ref[...], lb_ref[...]).astype(BF16)
    row = lax.broadcasted_iota(jnp.int32, (CHUNK, CHUNK), 0)
    col = lax.broadcasted_iota(jnp.int32, (CHUNK, CHUNK), 1)
    ws = [jnp.where(row >= col, ws_ref[g], 0.0).astype(BF16) for g in range(GM_GROUPS)]
    group = lax.broadcasted_iota(jnp.int32, (CHUNK, GM_WIDTH), 1) // GM_GROUP_DIM
    for c in range(ts // CHUNK):
        r0 = c * CHUNK
        v = gv[r0:r0 + CHUNK, :]
        s = _dot(ws[GM_GROUPS - 1], v)
        for g in range(GM_GROUPS - 2, -1, -1):
            s = jnp.where(group == g, _dot(ws[g], v), s)
        o_ref[0, r0:r0 + CHUNK, :] = (gu[r0:r0 + CHUNK, :] * (s + bs_ref[...])).astype(BF16)


def _gmlp(gm_in, ln_g, ln_b, w_s, b_s, *, ts):
    nb, t, _ = gm_in.shape
    bs = jnp.repeat(jnp.transpose(b_s), GM_GROUP_DIM, axis=1)
    row = lambda b, i: (b, i, 0)
    return pl.pallas_call(
        functools.partial(_gmlp_kernel, ts=ts),
        grid=(nb, t // ts),
        in_specs=[
            pl.BlockSpec((1, ts, 2 * GM_WIDTH), row),
            pl.BlockSpec((1, GM_WIDTH), lambda b, i: (0, 0)),
            pl.BlockSpec((1, GM_WIDTH), lambda b, i: (0, 0)),
            pl.BlockSpec((GM_GROUPS, CHUNK, CHUNK), lambda b, i: (0, 0, 0)),
            pl.BlockSpec((CHUNK, GM_WIDTH), lambda b, i: (0, 0)),
        ],
        out_specs=pl.BlockSpec((1, ts, GM_WIDTH), row),
        out_shape=jax.ShapeDtypeStruct((nb, t, GM_WIDTH), BF16),
        compiler_params=_cparams(("arbitrary", "arbitrary")),
        name="gmlp_gate",
    )(gm_in, ln_g.reshape(1, -1), ln_b.reshape(1, -1), w_s, bs)


def _mla_prep_kernel(mla_ref, misc_ref, qn_ref, kvn_ref, wqa_ref, wqb_ref, wk_ref, wv_ref,
                     cq_ref, sq_ref, ck_ref, sk_ref, q_ref, k_ref, v_ref):
    m = mla_ref[0].astype(F32)
    qn = (_rms(m[:, 0:MLA_Q_RANK]) * qn_ref[...]).astype(BF16)
    kvn = (_rms(m[:, MLA_Q_RANK:MLA_Q_RANK + MLA_KV_RANK]) * kvn_ref[...]).astype(BF16)
    qa = _dot(qn, wqa_ref[...])
    qb = _dot(qn, wqb_ref[...])
    kk = _dot(kvn, wk_ref[...])
    vv = _dot(kvn, wv_ref[...])
    misc = misc_ref[0]
    kr = pltpu.roll(misc, 64 - MISC_KPE, 1) * ck_ref[...] + pltpu.roll(misc, 64 - MISC_KPE_SW, 1) * sk_ref[...]
    cq = cq_ref[...]
    sq = sq_ref[...]
    ones_lane = lax.broadcasted_iota(jnp.int32, (vv.shape[0], LANES), 1) == MLA_V
    for h in range(MLA_HEADS):
        lo = LANES * h
        q_ref[0, h] = (qa[:, lo:lo + LANES] * cq + qb[:, lo:lo + LANES] * sq).astype(BF16)
        k_ref[0, h] = (kk[:, lo:lo + LANES] + kr).astype(BF16)
        v_ref[0, h] = jnp.where(ones_lane, 1.0, vv[:, lo:lo + LANES]).astype(BF16)


def _rope_tables(t):
    inv = ROPE_THETA ** (-jnp.arange(0, MLA_ROPE, 2, dtype=F32) / MLA_ROPE)
    ang = jnp.arange(t, dtype=F32)[:, None] * inv[None, :]
    cos, sin = jnp.cos(ang), jnp.sin(ang)
    z64 = jnp.zeros((t, 64), F32)
    z32 = jnp.zeros((t, 32), F32)
    one64 = jnp.ones((t, 64), F32)
    qscale = MLA_SCALE * math.log2(math.e)
    cq = jnp.concatenate([one64, cos, cos, z32], axis=1) * qscale
    sq = jnp.concatenate([z64, -sin, sin, z32], axis=1) * qscale
    ck = jnp.concatenate([z64, cos, cos, z32], axis=1)
    sk = jnp.concatenate([z64, -sin, sin, z32], axis=1)
    return cq, sq, ck, sk


def _mla_weights(w_qb, w_kvb):
    dq = MLA_NOPE + MLA_ROPE
    wq = w_qb.reshape(MLA_Q_RANK, MLA_HEADS, dq)
    nope, pe = wq[..., :MLA_NOPE], wq[..., MLA_NOPE:]
    pe_sw = jnp.concatenate([pe[..., MLA_ROPE // 2:], pe[..., :MLA_ROPE // 2]], axis=-1)
    zpad = jnp.zeros((MLA_Q_RANK, MLA_HEADS, LANES - dq), w_qb.dtype)
    wqa = jnp.concatenate([nope, pe, zpad], axis=-1).reshape(MLA_Q_RANK, MLA_HEADS * LANES)
    wqb = jnp.concatenate([jnp.zeros_like(nope), pe_sw, zpad], axis=-1).reshape(MLA_Q_RANK, MLA_HEADS * LANES)
    wkv = w_kvb.reshape(MLA_KV_RANK, MLA_HEADS, MLA_NOPE + MLA_V)
    wk = jnp.concatenate([wkv[..., :MLA_NOPE], jnp.zeros((MLA_KV_RANK, MLA_HEADS, LANES - MLA_NOPE), w_kvb.dtype)],
                         axis=-1).reshape(MLA_KV_RANK, MLA_HEADS * LANES)
    wv = jnp.concatenate([wkv[..., MLA_NOPE:], jnp.zeros((MLA_KV_RANK, MLA_HEADS, LANES - MLA_V), w_kvb.dtype)],
                         axis=-1).reshape(MLA_KV_RANK, MLA_HEADS * LANES)
    return wqa.astype(BF16), wqb.astype(BF16), wk.astype(BF16), wv.astype(BF16)


def _mla_prep(mla_in, misc, q_norm, kv_norm, w_qb, w_kvb, tables, *, tt):
    nb, t, _ = mla_in.shape
    wqa, wqb, wk, wv = _mla_weights(w_qb, w_kvb)
    row = lambda b, i: (b, i, 0)
    const2 = lambda b, i: (0, 0)
    tab = pl.BlockSpec((tt, LANES), lambda b, i: (i, 0))
    hrow = lambda b, i: (b, 0, i, 0)
    return pl.pallas_call(
        _mla_prep_kernel,
        grid=(nb, t // tt),
        in_specs=[
            pl.BlockSpec((1, tt, 384), row),
            pl.BlockSpec((1, tt, LANES), row),
            pl.BlockSpec((1, MLA_Q_RANK), const2),
            pl.BlockSpec((1, MLA_KV_RANK), const2),
            pl.BlockSpec((MLA_Q_RANK, MLA_HEADS * LANES), const2),
            pl.BlockSpec((MLA_Q_RANK, MLA_HEADS * LANES), const2),
            pl.BlockSpec((MLA_KV_RANK, MLA_HEADS * LANES), const2),
            pl.BlockSpec((MLA_KV_RANK, MLA_HEADS * LANES), const2),
            tab, tab, tab, tab,
        ],
        out_specs=[
            pl.BlockSpec((1, MLA_HEADS, tt, LANES), hrow),
            pl.BlockSpec((1, MLA_HEADS, tt, LANES), hrow),
            pl.BlockSpec((1, MLA_HEADS, tt, LANES), hrow),
        ],
        out_shape=[
            jax.ShapeDtypeStruct((nb, MLA_HEADS, t, LANES), BF16),
            jax.ShapeDtypeStruct((nb, MLA_HEADS, t, LANES), BF16),
            jax.ShapeDtypeStruct((nb, MLA_HEADS, t, LANES), BF16),
        ],
        compiler_params=_cparams(("arbitrary", "arbitrary")),
        name="mla_prep",
    )(mla_in, misc, q_norm.reshape(1, -1), kv_norm.reshape(1, -1), wqa, wqb, wk, wv, *tables)


def _flash_kernel(q_ref, k_ref, v_ref, o_ref, *, tq):
    i = pl.program_id(1)
    row = lax.broadcasted_iota(jnp.int32, (tq, tq), 0)
    col = lax.broadcasted_iota(jnp.int32, (tq, tq), 1)
    def step(j, carry, masked):
        start = pl.multiple_of(j * tq, tq)
        out = []
        for h in range(MLA_HEADS):
            m, acc = carry[h]
            kb = k_ref[0, h, pl.ds(start, tq), :]
            vb = v_ref[0, h, pl.ds(start, tq), :]
            s = _dot_nt(q_ref[0, h], kb)
            if masked:
                s = jnp.where(col <= row, s, -jnp.inf)
            m_new = jnp.maximum(m, jnp.max(s, axis=1, keepdims=True))
            p = jnp.exp2(s - m_new)
            acc = jnp.exp2(m - m_new) * acc + _dot(p.astype(BF16), vb)
            out.append((m_new, acc))
        return tuple(out)

    init = tuple((jnp.full((tq, 1), -jnp.inf, F32), jnp.zeros((tq, LANES), F32)) for _ in range(MLA_HEADS))
    carry = lax.fori_loop(0, i, functools.partial(step, masked=False), init)
    carry = step(i, carry, True)
    outs = [acc[:, 0:MLA_V] / acc[:, MLA_V:MLA_V + 1] for _, acc in carry]
    o_ref[0] = jnp.concatenate(outs, axis=-1).astype(BF16)


def _flash(q, k, v, *, tq):
    nb, nh, t, _ = q.shape
    return pl.pallas_call(
        functools.partial(_flash_kernel, tq=tq),
        grid=(nb, t // tq),
        in_specs=[
            pl.BlockSpec((1, nh, tq, LANES), lambda b, i: (b, 0, i, 0)),
            pl.BlockSpec((1, nh, t, LANES), lambda b, i: (b, 0, 0, 0)),
            pl.BlockSpec((1, nh, t, LANES), lambda b, i: (b, 0, 0, 0)),
        ],
        out_specs=pl.BlockSpec((1, tq, nh * MLA_V), lambda b, i: (b, i, 0)),
        out_shape=jax.ShapeDtypeStruct((nb, t, nh * MLA_V), BF16),
        compiler_params=_cparams(("arbitrary", "arbitrary")),
        name="mla_flash",
    )(q, k, v)


def _residual_ln(x, y, gate, g, b):
    return _layer_norm(DN_ALPHA * x + (1.0 + gate) * y, g, b)


def _outproj_kernel(*refs, route, tm):
    if route:
        (ys_ref, ym_ref, yg_ref, x_ref, mod1_ref, mod2_ref, w_ref, g_ref, b_ref, wr_ref,
         xo_ref, h_ref, rt_ref, cnt_ref, run_ref) = refs
    else:
        ys_ref, ym_ref, yg_ref, x_ref, mod1_ref, mod2_ref, w_ref, g_ref, b_ref, xo_ref, h_ref = refs
    y = (_dot(ys_ref[0], w_ref[0:512, :]) + _dot(ym_ref[0], w_ref[512:768, :])
         + _dot(yg_ref[0], w_ref[768:1024, :]))
    gate = mod1_ref[0][:, 2 * D_MODEL:3 * D_MODEL]
    xn = _residual_ln(x_ref[0], y, gate, g_ref[...], b_ref[...])
    xo_ref[0] = xn
    m2 = mod2_ref[0]
    h = xn * (1.0 + m2[:, D_MODEL:2 * D_MODEL]) + m2[:, 0:D_MODEL]
    if not route:
        h_ref[0] = h.astype(BF16)
        return
    h_ref[0] = h

    @pl.when((pl.program_id(0) == 0) & (pl.program_id(1) == 0))
    def _():
        run_ref[...] = jnp.zeros(run_ref.shape, F32)

    lane = lax.broadcasted_iota(jnp.int32, (tm, LANES), 1)
    logits = jnp.where(lane < N_EXPERTS, _dot(h.astype(BF16), wr_ref[...]), -jnp.inf)
    v1 = jnp.max(logits, axis=1, keepdims=True)
    i1 = jnp.min(jnp.where(logits == v1, lane, LANES), axis=1, keepdims=True)
    rest = jnp.where(lane == i1, -jnp.inf, logits)
    v2 = jnp.max(rest, axis=1, keepdims=True)
    i2 = jnp.min(jnp.where(rest == v2, lane, LANES), axis=1, keepdims=True)
    e2 = jnp.exp(v2 - v1)
    g1 = 1.0 / (1.0 + e2)
    g2 = e2 / (1.0 + e2)
    hot = ((lane == i1) | (lane == i2)).astype(F32)
    r_i = lax.broadcasted_iota(jnp.int32, (tm, tm), 0)
    c_i = lax.broadcasted_iota(jnp.int32, (tm, tm), 1)
    below = (c_i < r_i).astype(BF16)
    rank = _dot(below, hot.astype(BF16)) + run_ref[...]
    r1 = jnp.sum(jnp.where(lane == i1, rank, 0.0), axis=1, keepdims=True)
    r2 = jnp.sum(jnp.where(lane == i2, rank, 0.0), axis=1, keepdims=True)
    run = run_ref[...] + jnp.sum(hot, axis=0, keepdims=True)
    run_ref[...] = run
    cnt_ref[...] = jnp.broadcast_to(run, cnt_ref.shape)
    rt = jnp.where(lane == 0, i1.astype(F32), 0.0)
    rt = jnp.where(lane == 1, i2.astype(F32), rt)
    rt = jnp.where(lane == 2, r1, rt)
    rt = jnp.where(lane == 3, r2, rt)
    rt = jnp.where(lane == 4, g1, rt)
    rt = jnp.where(lane == 5, g2, rt)
    rt_ref[0] = rt


def _outproj(y_ssd, y_mla, y_gm, x, mod1, mod2, w_out, ln_g, ln_b, w_router=None, *, tm):
    nb, t, _ = x.shape
    route = w_router is not None
    row = lambda b, i: (b, i, 0)
    const2 = lambda b, i: (0, 0)
    modspec = pl.BlockSpec((1, 1, 3 * D_MODEL), lambda b, i: (b, 0, 0))
    in_specs = [
        pl.BlockSpec((1, tm, 512), row), pl.BlockSpec((1, tm, 256), row), pl.BlockSpec((1, tm, 256), row),
        pl.BlockSpec((1, tm, D_MODEL), row), modspec, modspec,
        pl.BlockSpec((D_MODEL, D_MODEL), const2),
        pl.BlockSpec((1, D_MODEL), const2), pl.BlockSpec((1, D_MODEL), const2),
    ]
    args = [y_ssd, y_mla, y_gm, x, mod1, mod2, w_out.astype(BF16), ln_g.reshape(1, -1), ln_b.reshape(1, -1)]
    out_specs = [pl.BlockSpec((1, tm, D_MODEL), row), pl.BlockSpec((1, tm, D_MODEL), row)]
    out_shape = [jax.ShapeDtypeStruct((nb, t, D_MODEL), F32),
                 jax.ShapeDtypeStruct((nb, t, D_MODEL), F32 if route else BF16)]
    scratch = []
    if route:
        wr = jnp.pad(w_router, ((0, 0), (0, LANES - N_EXPERTS))).astype(BF16)
        in_specs.append(pl.BlockSpec((D_MODEL, LANES), const2))
        args.append(wr)
        out_specs += [pl.BlockSpec((1, tm, LANES), row), pl.BlockSpec((8, LANES), const2)]
        out_shape += [jax.ShapeDtypeStruct((nb, t, LANES), F32), jax.ShapeDtypeStruct((8, LANES), F32)]
        scratch = [pltpu.VMEM((1, LANES), F32)]
    return pl.pallas_call(
        functools.partial(_outproj_kernel, route=route, tm=tm),
        grid=(nb, t // tm),
        in_specs=in_specs, out_specs=out_specs, out_shape=out_shape, scratch_shapes=scratch,
        compiler_params=_cparams(("arbitrary", "arbitrary")),
        name="outproj_route" if route else "outproj",
    )(*args)


def _swiglu_acc(h, w1_ref, w3_ref, w2_ref, widx, side_work=None):
    acc = None
    for j in range(D_FF // FF_CHUNK):
        lo = j * FF_CHUNK
        if side_work is not None:
            side_work(j)
        a = _dot(h, w1_ref[widx + (slice(None), slice(lo, lo + FF_CHUNK))])
        g = _dot(h, w3_ref[widx + (slice(None), slice(lo, lo + FF_CHUNK))])
        p = (_silu(a) * g).astype(BF16)
        part = _dot(p, w2_ref[widx + (slice(lo, lo + FF_CHUNK), slice(None))])
        acc = part if acc is None else acc + part
    return acc


def _ffn_kernel(h_ref, x_ref, mod_ref, w1_ref, w3_ref, w2_ref, g_ref, b_ref, o_ref):
    y = _swiglu_acc(h_ref[0], w1_ref, w3_ref, w2_ref, ())
    gate = mod_ref[0][:, 2 * D_MODEL:3 * D_MODEL]
    o_ref[0] = _residual_ln(x_ref[0], y, gate, g_ref[...], b_ref[...])


def _ffn(h, x, mod, w1, w3, w2, ln_g, ln_b, *, tm):
    nb, t, _ = x.shape
    row = lambda b, i: (b, i, 0)
    const2 = lambda b, i: (0, 0)
    wspec = lambda shape: pl.BlockSpec(shape, const2, pipeline_mode=pl.Buffered(1))
    return pl.pallas_call(
        _ffn_kernel,
        grid=(nb, t // tm),
        in_specs=[
            pl.BlockSpec((1, tm, D_MODEL), row), pl.BlockSpec((1, tm, D_MODEL), row),
            pl.BlockSpec((1, 1, 3 * D_MODEL), lambda b, i: (b, 0, 0)),
            wspec((D_MODEL, D_FF)), wspec((D_MODEL, D_FF)), wspec((D_FF, D_MODEL)),
            pl.BlockSpec((1, D_MODEL), const2), pl.BlockSpec((1, D_MODEL), const2),
        ],
        out_specs=pl.BlockSpec((1, tm, D_MODEL), row),
        out_shape=jax.ShapeDtypeStruct((nb, t, D_MODEL), F32),
        compiler_params=_cparams(("arbitrary", "arbitrary")),
        name="ffn_dense",
    )(h, x, mod, w1.astype(BF16), w3.astype(BF16), w2.astype(BF16), ln_g.reshape(1, -1), ln_b.reshape(1, -1))


def _invert_kernel(slot_ref, init_ref, inv_ref, sem, *, n_pairs):
    cp = pltpu.make_async_copy(init_ref, inv_ref, sem)
    cp.start()
    cp.wait()

    def body(i, c):
        inv_ref[slot_ref[i]] = i
        return c

    lax.fori_loop(0, n_pairs, body, 0, unroll=8)


def _moe_invert(slots, inv_init):
    return pl.pallas_call(
        functools.partial(_invert_kernel, n_pairs=slots.shape[0]),
        grid_spec=pltpu.PrefetchScalarGridSpec(
            num_scalar_prefetch=1,
            grid=(1,),
            in_specs=[pl.BlockSpec(memory_space=pl.ANY)],
            out_specs=pl.BlockSpec(memory_space=pltpu.SMEM),
            scratch_shapes=[pltpu.SemaphoreType.DMA(())],
        ),
        out_shape=jax.ShapeDtypeStruct(inv_init.shape, jnp.int32),
        compiler_params=_cparams(("arbitrary",)),
        name="moe_invert",
    )(slots, inv_init)


def _moe_ffn_kernel(te_ref, inv_ref, h_hbm, w1_ref, w3_ref, w2_ref, y_hbm, xbuf, obuf, gsem, ssem, *, tg, n_tok):
    del te_ref
    j = pl.program_id(0)
    last = pl.num_programs(0) - 1
    slot = lax.rem(j, 2)
    other = 1 - slot

    def gather_row(tile, r, sl):
        tok = inv_ref[(tile + 1) * tg + r] & (n_tok - 1)
        return pltpu.make_async_copy(h_hbm.at[pl.ds(tok, 1), :], xbuf.at[sl, pl.ds(r, 1), :], gsem.at[sl])

    def scatter_row(tile, r, sl):
        dst = inv_ref[(tile + 1) * tg + r]
        return pltpu.make_async_copy(obuf.at[sl, pl.ds(r, 1), :], y_hbm.at[pl.ds(dst, 1), :], ssem.at[sl])

    def wait_gather(sl):
        pltpu.make_async_copy(h_hbm.at[pl.ds(0, tg), :], xbuf.at[sl], gsem.at[sl]).wait()

    def wait_scatter(sl):
        pltpu.make_async_copy(obuf.at[sl], y_hbm.at[pl.ds(0, tg), :], ssem.at[sl]).wait()

    @pl.when(j == 0)
    def _():
        obuf[1] = jnp.zeros((tg, D_MODEL), F32)

        def first(r, c):
            gather_row(0, r, 0).start()
            return c

        lax.fori_loop(0, tg, first, 0)

    @pl.when(j > 0)
    def _():
        wait_scatter(slot)

    wait_gather(slot)
    x = xbuf[slot].astype(BF16)
    nxt = jnp.minimum(j + 1, last)
    n_chunks = D_FF // FF_CHUNK
    per = -(-tg // n_chunks)

    def side_work(c):
        for r in range(c * per, min((c + 1) * per, tg)):
            gather_row(nxt, r, other).start()
            scatter_row(j - 1, r, other).start()

    obuf[slot] = _swiglu_acc(x, w1_ref, w3_ref, w2_ref, (0,), side_work)

    @pl.when(j == last)
    def _():
        wait_gather(other)
        wait_scatter(other)

        def final(r, c):
            scatter_row(j, r, slot).start()
            return c

        lax.fori_loop(0, tg, final, 0)
        wait_scatter(slot)


def _moe_ffn(tile_expert, inv, h, w1, w3, w2, *, tg, n_out):
    n_tiles = tile_expert.shape[0]
    wmap = lambda j, te, iv: (te[j], 0, 0)
    return pl.pallas_call(
        functools.partial(_moe_ffn_kernel, tg=tg, n_tok=h.shape[0]),
        grid_spec=pltpu.PrefetchScalarGridSpec(
            num_scalar_prefetch=2,
            grid=(n_tiles,),
            in_specs=[
                pl.BlockSpec(memory_space=pl.ANY),
                pl.BlockSpec((1, D_MODEL, D_FF), wmap),
                pl.BlockSpec((1, D_MODEL, D_FF), wmap),
                pl.BlockSpec((1, D_FF, D_MODEL), wmap),
            ],
            out_specs=pl.BlockSpec(memory_space=pl.ANY),
            scratch_shapes=[
                pltpu.VMEM((2, tg, D_MODEL), F32), pltpu.VMEM((2, tg, D_MODEL), F32),
                pltpu.SemaphoreType.DMA((2,)), pltpu.SemaphoreType.DMA((2,)),
            ],
        ),
        out_shape=jax.ShapeDtypeStruct((n_out, D_MODEL), F32),
        compiler_params=_cparams(("arbitrary",)),
        name="moe_ffn",
    )(tile_expert, inv, h, w1.astype(BF16), w3.astype(BF16), w2.astype(BF16))


def _combine_kernel(y1_ref, y2_ref, rt_ref, x_ref, mod_ref, g_ref, b_ref, o_ref):
    rt = rt_ref[0]
    y = rt[:, 4:5] * y1_ref[...] + rt[:, 5:6] * y2_ref[...]
    gate = mod_ref[0][:, 2 * D_MODEL:3 * D_MODEL]
    o_ref[0] = _residual_ln(x_ref[0], y, gate, g_ref[...], b_ref[...])


def _combine(y2, route, x, mod, ln_g, ln_b, *, tm):
    nb, t, _ = x.shape
    row = lambda b, i: (b, i, 0)
    const2 = lambda b, i: (0, 0)
    nt = t // tm
    return pl.pallas_call(
        _combine_kernel,
        grid=(nb, nt),
        in_specs=[
            pl.BlockSpec((tm, D_MODEL), lambda b, i: (b * nt + i, 0)),
            pl.BlockSpec((tm, D_MODEL), lambda b, i: (nb * nt + b * nt + i, 0)),
            pl.BlockSpec((1, tm, LANES), row),
            pl.BlockSpec((1, tm, D_MODEL), row),
            pl.BlockSpec((1, 1, 3 * D_MODEL), lambda b, i: (b, 0, 0)),
            pl.BlockSpec((1, D_MODEL), const2), pl.BlockSpec((1, D_MODEL), const2),
        ],
        out_specs=pl.BlockSpec((1, tm, D_MODEL), row),
        out_shape=jax.ShapeDtypeStruct((nb, t, D_MODEL), F32),
        compiler_params=_cparams(("arbitrary", "arbitrary")),
        name="moe_combine",
    )(y2, y2, route, x, mod, ln_g.reshape(1, -1), ln_b.reshape(1, -1))


def _moe_plan(route, counts, *, tg, n_tiles):
    n_pairs = 2 * route.shape[0] * route.shape[1]
    cnt = counts[0, :N_EXPERTS].astype(jnp.int32)
    tiles = (cnt + tg - 1) // tg
    tile_end = jnp.cumsum(tiles)
    start = (tile_end - tiles) * tg
    r = route.reshape(-1, LANES)
    e = r[:, 0:2].astype(jnp.int32)
    assert (n_pairs // 2) & (n_pairs // 2 - 1) == 0, "token count must be a power of two (pair ids are masked)"
    slots = jnp.transpose(start[e] + r[:, 2:4].astype(jnp.int32)).reshape(-1) + tg
    n_active = tile_end[-1]
    j = jnp.minimum(jnp.arange(n_tiles, dtype=jnp.int32), n_active - 1)
    tile_expert = jnp.sum((j[:, None] >= tile_end[None, :]).astype(jnp.int32), axis=1)
    s = jnp.arange(n_tiles * tg, dtype=jnp.int32)
    es = jnp.minimum(jnp.sum((s[:, None] >= (tile_end * tg)[None, :]).astype(jnp.int32), axis=1), N_EXPERTS - 1)
    pad_before = jnp.cumsum(tiles * tg - cnt) - (tiles * tg - cnt)
    in_group = s - start[es]
    pad_id = jnp.where(s >= n_active * tg, s - n_pairs, pad_before[es] + in_group - cnt[es])
    n_spare = n_tiles * tg - n_pairs
    body = n_pairs + jnp.clip(pad_id, 0, n_spare - 1)
    head = n_pairs + n_spare + jnp.arange(tg, dtype=jnp.int32)
    return slots, tile_expert, jnp.concatenate([head, body]), n_pairs + n_spare + tg


def _w_in_padded(w_in):
    z, xs, bm, cm = (0, 512), (512, 1024), (1024, 1152), (1152, 1280)
    dt, ql, kvl, kpe, gu, gv = (1280, 1288), (1288, 1544), (1544, 1672), (1672, 1704), (1704, 1960), (1960, 2216)
    seg = lambda ab: w_in[:, ab[0]:ab[1]]
    k_pe = seg(kpe)
    k_sw = jnp.concatenate([k_pe[:, MLA_ROPE // 2:], k_pe[:, :MLA_ROPE // 2]], axis=1)
    pad = jnp.zeros((D_MODEL, LANES - 2 * MLA_ROPE - SSD_HEADS), w_in.dtype)
    cols = [seg(z), seg(xs), seg(bm), seg(cm), seg(ql), seg(kvl), seg(gu), seg(gv), k_pe, k_sw, seg(dt), pad]
    return jnp.concatenate(cols, axis=1).astype(BF16)


def _forward(x, c, ln0_g, ln0_b, ada_w, ada_b, post_ln_g, post_ln_b, w_in, ssd_conv_w, ssd_conv_b,
             ssd_dt_bias, ssd_a_log, ssd_d, ssd_norm_w, mla_q_norm, mla_w_qb, mla_kv_norm, mla_w_kvb,
             gm_ln_g, gm_ln_b, gm_w_s, gm_b_s, w_out, ffn_w1, ffn_w3, ffn_w2,
             moe_router, moe_w1, moe_w3, moe_w2, *, tm, tq, tg, tc):
    nb, t, _ = x.shape
    mods = _modulations(c, ada_w, ada_b).reshape(2 * DEPTH, nb, 1, 3 * D_MODEL)
    tables = _rope_tables(t)
    for layer in range(DEPTH):
        mod_mix, mod_ffn = mods[2 * layer], mods[2 * layer + 1]
        w_pad = _w_in_padded(w_in[layer])
        if layer == 0:
            x, p_ssd, p_mla, p_gm, p_misc = _inproj(x, mod_mix, w_pad, ln0_g, ln0_b, tm=tm)
        else:
            p_ssd, p_mla, p_gm, p_misc = _inproj(x, mod_mix, w_pad, tm=tm)
        y_ssd = _ssd(p_ssd, p_misc, ssd_conv_w[layer], ssd_conv_b[layer], ssd_dt_bias[layer],
                     ssd_a_log[layer], ssd_d[layer], ssd_norm_w[layer], ts=tm)
        q, k, v = _mla_prep(p_mla, p_misc, mla_q_norm[layer], mla_kv_norm[layer], mla_w_qb[layer],
                            mla_w_kvb[layer], tables, tt=tm)
        y_mla = _flash(q, k, v, tq=tq)
        y_gm = _gmlp(p_gm, gm_ln_g[layer], gm_ln_b[layer], gm_w_s[layer], gm_b_s[layer], ts=tm)
        i = layer // 2
        if layer % 2 == 0:
            x, h = _outproj(y_ssd, y_mla, y_gm, x, mod_mix, mod_ffn, w_out[layer],
                            post_ln_g[layer, 0], post_ln_b[layer, 0], tm=tm)
            x = _ffn(h, x, mod_ffn, ffn_w1[i], ffn_w3[i], ffn_w2[i], post_ln_g[layer, 1], post_ln_b[layer, 1], tm=tm)
        else:
            x, h, route, counts = _outproj(y_ssd, y_mla, y_gm, x, mod_mix, mod_ffn, w_out[layer],
                                           post_ln_g[layer, 0], post_ln_b[layer, 0], moe_router[i], tm=tm)
            n_tiles = (2 * nb * t) // tg + N_EXPERTS
            slots, tile_expert, inv_init, n_out = _moe_plan(route, counts, tg=tg, n_tiles=n_tiles)
            inv = _moe_invert(slots, inv_init)
            y2 = _moe_ffn(tile_expert, inv, h.reshape(nb * t, D_MODEL), moe_w1[i], moe_w3[i], moe_w2[i],
                          tg=tg, n_out=n_out)
            x = _combine(y2, route, x, mod_ffn, post_ln_g[layer, 1], post_ln_b[layer, 1], tm=tc)
    return x


def kernel(x, c, ln0_g, ln0_b, ada_w, ada_b, post_ln_g, post_ln_b, w_in, ssd_conv_w, ssd_conv_b, ssd_dt_bias, ssd_a_log, ssd_d, ssd_norm_w, mla_q_norm, mla_w_qb, mla_kv_norm, mla_w_kvb, gm_ln_g, gm_ln_b, gm_w_s, gm_b_s, w_out, ffn_w1, ffn_w3, ffn_w2, moe_router, moe_w1, moe_w3, moe_w2):
    t = x.shape[1]
    tm = min(512, t)
    return _forward(x, c, ln0_g, ln0_b, ada_w, ada_b, post_ln_g, post_ln_b, w_in, ssd_conv_w, ssd_conv_b,
                    ssd_dt_bias, ssd_a_log, ssd_d, ssd_norm_w, mla_q_norm, mla_w_qb, mla_kv_norm, mla_w_kvb,
                    gm_ln_g, gm_ln_b, gm_w_s, gm_b_s, w_out, ffn_w1, ffn_w3, ffn_w2,
                    moe_router, moe_w1, moe_w3, moe_w2, tm=tm, tq=tm, tg=tm, tc=tm)
```

```python
import functools
import math

import numpy as np
import jax
import jax.numpy as jnp
from jax import lax
from jax.experimental import pallas as pl
from jax.experimental.pallas import tpu as pltpu

F32 = jnp.float32
BF16 = jnp.bfloat16

D_MODEL = 1024
DEPTH = 2
SSD_INNER = 512
SSD_HEAD_DIM = 64
SSD_HEADS = 8
SSD_GROUPS = 2
SSD_STATE = 64
SSD_CONV = 4
CHUNK = 128
SSD_CONV_DIM = 768
MLA_HEADS = 4
MLA_Q_RANK = 256
MLA_KV_RANK = 128
MLA_NOPE = 64
MLA_ROPE = 32
MLA_V = 64
MLA_SCALE = (MLA_NOPE + MLA_ROPE) ** -0.5
ROPE_THETA = 10000.0
GM_GROUPS = 4
GM_GROUP_DIM = 64
GM_WIDTH = 256
D_FF = 2816
N_EXPERTS = 8
DN_ALPHA = (2 * DEPTH) ** 0.25
LN_EPS = 1e-5
RMS_EPS = 1e-6

LANES = 128
FF_CHUNK = 256
VMEM_LIMIT = 56 * 1024 * 1024

MISC_KPE = 0
MISC_KPE_SW = 32
MISC_DT = 64
D_IN_PAD = 1280 + 384 + 512 + 128


def _cparams(sem):
    return pltpu.CompilerParams(dimension_semantics=sem, vmem_limit_bytes=VMEM_LIMIT)


def _silu(x):
    return x * jax.nn.sigmoid(x)


def _gelu(x):
    return 0.5 * x * (1.0 + lax.erf(x * (2.0 ** -0.5)))


def _softplus(x):
    return jnp.maximum(x, 0.0) + jnp.log1p(jnp.exp(-jnp.abs(x)))


def _layer_norm(x, g, b):
    mu = jnp.mean(x, axis=-1, keepdims=True)
    xc = x - mu
    var = jnp.mean(xc * xc, axis=-1, keepdims=True)
    return xc * lax.rsqrt(var + LN_EPS) * g + b


def _rms(x):
    return x * lax.rsqrt(jnp.mean(x * x, axis=-1, keepdims=True) + RMS_EPS)


def _dot(a, b):
    return jnp.dot(a, b, preferred_element_type=F32)


def _dot_nt(a, b):
    return lax.dot_general(a, b, (((1,), (1,)), ((), ())), preferred_element_type=F32)


def _dot_tn(a, b):
    return lax.dot_general(a, b, (((0,), (0,)), ((), ())), preferred_element_type=F32)


def _mod_kernel(c_ref, w_ref, b_ref, o_ref):
    s = _silu(c_ref[...]).astype(BF16)
    o_ref[0] = _dot(s, w_ref[0].astype(BF16)) + b_ref[0]


def _modulations(c, ada_w, ada_b):
    nb = c.shape[0]
    n = ada_w.shape[0] * ada_w.shape[1]
    w = ada_w.reshape(n, D_MODEL, 3 * D_MODEL)
    b = ada_b.reshape(n, 1, 3 * D_MODEL)
    tn = 1024
    return pl.pallas_call(
        _mod_kernel,
        grid=(n, 3 * D_MODEL // tn),
        in_specs=[
            pl.BlockSpec((nb, D_MODEL), lambda j, k: (0, 0)),
            pl.BlockSpec((1, D_MODEL, tn), lambda j, k: (j, 0, k)),
            pl.BlockSpec((1, 1, tn), lambda j, k: (j, 0, k)),
        ],
        out_specs=pl.BlockSpec((1, nb, tn), lambda j, k: (j, 0, k)),
        out_shape=jax.ShapeDtypeStruct((n, nb, 3 * D_MODEL), F32),
        compiler_params=_cparams(("arbitrary", "arbitrary")),
        name="adaln_mod",
    )(c, w, b)


def _inproj_kernel(*refs, do_ln):
    if do_ln:
        x_ref, mod_ref, g_ref, b_ref, w_ref, xo_ref, ssd_ref, mla_ref, gm_ref, misc_ref = refs
    else:
        x_ref, mod_ref, w_ref, ssd_ref, mla_ref, gm_ref, misc_ref = refs
    x = x_ref[0]
    if do_ln:
        x = _layer_norm(x, g_ref[...], b_ref[...])
        xo_ref[0] = x
    m = mod_ref[0]
    h = x * (1.0 + m[:, D_MODEL:2 * D_MODEL]) + m[:, 0:D_MODEL]
    p = _dot(h.astype(BF16), w_ref[...])
    ssd_ref[0] = p[:, 0:1280].astype(BF16)
    mla_ref[0] = p[:, 1280:1664].astype(BF16)
    gm_ref[0] = p[:, 1664:2176].astype(BF16)
    misc_ref[0] = p[:, 2176:2304]


def _inproj(x, mod, w_pad, ln_g=None, ln_b=None, *, tm):
    nb, t, _ = x.shape
    do_ln = ln_g is not None
    row = lambda b, i: (b, i, 0)
    const2 = lambda b, i: (0, 0)
    in_specs = [pl.BlockSpec((1, tm, D_MODEL), row), pl.BlockSpec((1, 1, 3 * D_MODEL), lambda b, i: (b, 0, 0))]
    args = [x, mod]
    if do_ln:
        in_specs += [pl.BlockSpec((1, D_MODEL), const2), pl.BlockSpec((1, D_MODEL), const2)]
        args += [ln_g.reshape(1, D_MODEL), ln_b.reshape(1, D_MODEL)]
    in_specs.append(pl.BlockSpec((D_MODEL, D_IN_PAD), const2))
    args.append(w_pad)
    out_specs = [pl.BlockSpec((1, tm, 1280), row), pl.BlockSpec((1, tm, 384), row),
                 pl.BlockSpec((1, tm, 512), row), pl.BlockSpec((1, tm, LANES), row)]
    out_shape = [jax.ShapeDtypeStruct((nb, t, 1280), BF16), jax.ShapeDtypeStruct((nb, t, 384), BF16),
                 jax.ShapeDtypeStruct((nb, t, 512), BF16), jax.ShapeDtypeStruct((nb, t, LANES), F32)]
    if do_ln:
        out_specs = [pl.BlockSpec((1, tm, D_MODEL), row)] + out_specs
        out_shape = [jax.ShapeDtypeStruct((nb, t, D_MODEL), F32)] + out_shape
    return pl.pallas_call(
        functools.partial(_inproj_kernel, do_ln=do_ln),
        grid=(nb, t // tm),
        in_specs=in_specs, out_specs=out_specs, out_shape=out_shape,
        compiler_params=_cparams(("arbitrary", "arbitrary")),
        name="inproj_ln" if do_ln else "inproj",
    )(*args)


def _ssd_kernel(ssd_ref, misc_ref, cw_ref, cb_ref, dtb_ref, arow_ref, drow_ref, nw_ref, o_ref,
                xpad_ref, st_ref, y_ref, *, ts):
    L = CHUNK

    @pl.when(pl.program_id(1) == 0)
    def _():
        xpad_ref[0:8, :] = jnp.zeros((8, SSD_CONV_DIM), F32)
        st_ref[...] = jnp.zeros(st_ref.shape, F32)

    blk = ssd_ref[0]
    xpad_ref[8:8 + ts, :] = blk[:, SSD_INNER:SSD_INNER + SSD_CONV_DIM].astype(F32)
    cw = cw_ref[...]
    acc = cb_ref[...] + cw[3:4, :] * xpad_ref[8:8 + ts, :]
    for k in range(SSD_CONV - 1):
        acc = acc + cw[k:k + 1, :] * xpad_ref[pl.ds(5 + k, ts), :]
    xpad_ref[0:8, :] = xpad_ref[ts:ts + 8, :]
    xbc = _silu(acc)

    dtv_all = _softplus(misc_ref[0] + dtb_ref[...])
    a_all = dtv_all * arow_ref[...]
    row = lax.broadcasted_iota(jnp.int32, (L, LANES), 0)
    col = lax.broadcasted_iota(jnp.int32, (L, LANES), 1)
    causal = row >= col
    lo_half = col < SSD_HEAD_DIM
    lo_half_row = lo_half[0:1, :]
    heads_per_group = SSD_HEADS // SSD_GROUPS

    for c in range(ts // L):
        r0 = c * L
        xs = xbc[r0:r0 + L, 0:SSD_INNER]
        bm = xbc[r0:r0 + L, SSD_INNER:SSD_INNER + 128]
        cm = xbc[r0:r0 + L, SSD_INNER + 128:SSD_INNER + 256]
        dtv = dtv_all[r0:r0 + L, :]
        acs = a_all[r0:r0 + L, :]
        s = 1
        while s < L:
            acs = acs + jnp.where(row >= s, pltpu.roll(acs, s, 0), 0.0)
            s *= 2
        acs_t = acs.T
        atot = acs[L - 1:L, :]
        e_out = jnp.exp(acs)
        e_end = jnp.exp(atot - acs)
        e_tot = jnp.exp(atot)

        for g in range(SSD_GROUPS):
            in_group = (col >= SSD_STATE * g) & (col < SSD_STATE * (g + 1))
            bg = jnp.where(in_group, bm, 0.0).astype(BF16)
            cg = jnp.where(in_group, cm, 0.0).astype(BF16)
            cb = _dot_nt(cg, bg)
            for q in range(heads_per_group // 2):
                pair = g * (heads_per_group // 2) + q
                l0 = MISC_DT + 2 * pair
                l1 = l0 + 1
                lanes = slice(LANES * pair, LANES * pair + LANES)

                def pair_cols(arr):
                    return jnp.where(lo_half, arr[:, l0:l0 + 1], arr[:, l1:l1 + 1])

                x_p = xs[:, lanes]
                xdt = x_p * pair_cols(dtv)
                xdt_b = xdt.astype(BF16)
                halves = []
                for ll in (l0, l1):
                    decay = jnp.exp(jnp.where(causal, acs[:, ll:ll + 1] - acs_t[ll:ll + 1, :], -jnp.inf))
                    halves.append(_dot((cb * decay).astype(BF16), xdt_b))
                y = jnp.where(lo_half, halves[0], halves[1])
                st = st_ref[pair]
                y = y + pair_cols(e_out) * _dot(cg, st.astype(BF16)) + x_p * drow_ref[:, lanes]
                wst = (xdt * pair_cols(e_end)).astype(BF16)
                tot = jnp.where(lo_half_row, e_tot[:, l0:l0 + 1], e_tot[:, l1:l1 + 1])
                st_ref[pair] = tot * st + _dot_tn(bg, wst)
                y_ref[r0:r0 + L, lanes] = y

    gy = y_ref[...] * _silu(blk[:, 0:SSD_INNER].astype(F32))
    half = SSD_INNER // SSD_GROUPS
    for g in range(SSD_GROUPS):
        seg = gy[:, half * g:half * g + half]
        o_ref[0, :, half * g:half * g + half] = (_rms(seg) * nw_ref[:, half * g:half * g + half]).astype(BF16)


def _ssd(ssd_in, misc, conv_w, conv_b, dt_bias, a_log, d_skip, norm_w, *, ts):
    nb, t, _ = ssd_in.shape
    lane_pad = (MISC_DT, LANES - MISC_DT - SSD_HEADS)
    dtb = jnp.pad(dt_bias.astype(F32), lane_pad).reshape(1, LANES)
    arow = jnp.pad(-jnp.exp(a_log.astype(F32)), lane_pad).reshape(1, LANES)
    drow = jnp.repeat(d_skip.astype(F32), SSD_HEAD_DIM).reshape(1, SSD_INNER)
    row = lambda b, i: (b, i, 0)
    const2 = lambda b, i: (0, 0)
    return pl.pallas_call(
        functools.partial(_ssd_kernel, ts=ts),
        grid=(nb, t // ts),
        in_specs=[
            pl.BlockSpec((1, ts, 1280), row),
            pl.BlockSpec((1, ts, LANES), row),
            pl.BlockSpec((SSD_CONV, SSD_CONV_DIM), const2),
            pl.BlockSpec((1, SSD_CONV_DIM), const2),
            pl.BlockSpec((1, LANES), const2),
            pl.BlockSpec((1, LANES), const2),
            pl.BlockSpec((1, SSD_INNER), const2),
            pl.BlockSpec((1, SSD_INNER), const2),
        ],
        out_specs=pl.BlockSpec((1, ts, SSD_INNER), row),
        out_shape=jax.ShapeDtypeStruct((nb, t, SSD_INNER), BF16),
        scratch_shapes=[
            pltpu.VMEM((ts + 8, SSD_CONV_DIM), F32),
            pltpu.VMEM((SSD_HEADS // 2, SSD_GROUPS * SSD_STATE, 2 * SSD_HEAD_DIM), F32),
            pltpu.VMEM((ts, SSD_INNER), F32),
        ],
        compiler_params=_cparams(("arbitrary", "arbitrary")),
        name="ssd_scan",
    )(ssd_in, misc, conv_w, conv_b.reshape(1, -1), dtb, arow, drow, norm_w.reshape(1, -1))


def _gmlp_kernel(gm_ref, lg_ref, lb_ref, ws_ref, bs_ref, o_ref, *, ts):
    blk = gm_ref[0].astype(F32)
    gu = _gelu(blk[:, 0:GM_WIDTH])
    gv = _layer_norm(_gelu(blk[:, GM_WIDTH:2 * GM_WIDTH]), lg_ref[...], lb_ref[...]).astype(BF16)
    row = lax.broadcasted_iota(jnp.int32, (CHUNK, CHUNK), 0)
    col = lax.broadcasted_iota(jnp.int32, (CHUNK, CHUNK), 1)
    ws = [jnp.where(row >= col, ws_ref[g], 0.0).astype(BF16) for g in range(GM_GROUPS)]
    group = lax.broadcasted_iota(jnp.int32, (CHUNK, GM_WIDTH), 1) // GM_GROUP_DIM
    for c in range(ts // CHUNK):
        r0 = c * CHUNK
        v = gv[r0:r0 + CHUNK, :]
        s = _dot(ws[GM_GROUPS - 1], v)
        for g in range(GM_GROUPS - 2, -1, -1):
            s = jnp.where(group == g, _dot(ws[g], v), s)
        o_ref[0, r0:r0 + CHUNK, :] = (gu[r0:r0 + CHUNK, :] * (s + bs_ref[...])).astype(BF16)


def _gmlp(gm_in, ln_g, ln_b, w_s, b_s, *, ts):
    nb, t, _ = gm_in.shape
    bs = jnp.repeat(jnp.transpose(b_s), GM_GROUP_DIM, axis=1)
    row = lambda b, i: (b, i, 0)
    return pl.pallas_call(
        functools.partial(_gmlp_kernel, ts=ts),
        grid=(nb, t // ts),
        in_specs=[
            pl.BlockSpec((1, ts, 2 * GM_WIDTH), row),
            pl.BlockSpec((1, GM_WIDTH), lambda b, i: (0, 0)),
            pl.BlockSpec((1, GM_WIDTH), lambda b, i: (0, 0)),
            pl.BlockSpec((GM_GROUPS, CHUNK, CHUNK), lambda b, i: (0, 0, 0)),
            pl.BlockSpec((CHUNK, GM_WIDTH), lambda b, i: (0, 0)),
        ],
        out_specs=pl.BlockSpec((1, ts, GM_WIDTH), row),
        out_shape=jax.ShapeDtypeStruct((nb, t, GM_WIDTH), BF16),
        compiler_params=_cparams(("arbitrary", "arbitrary")),
        name="gmlp_gate",
    )(gm_in, ln_g.reshape(1, -1), ln_b.reshape(1, -1), w_s, bs)


def _mla_prep_kernel(mla_ref, misc_ref, qn_ref, kvn_ref, wqa_ref, wqb_ref, wk_ref, wv_ref,
                     cq_ref, sq_ref, ck_ref, sk_ref, q_ref, k_ref, v_ref):
    m = mla_ref[0].astype(F32)
    qn = (_rms(m[:, 0:MLA_Q_RANK]) * qn_ref[...]).astype(BF16)
    kvn = (_rms(m[:, MLA_Q_RANK:MLA_Q_RANK + MLA_KV_RANK]) * kvn_ref[...]).astype(BF16)
    qa = _dot(qn, wqa_ref[...])
    qb = _dot(qn, wqb_ref[...])
    kk = _dot(kvn, wk_ref[...])
    vv = _dot(kvn, wv_ref[...])
    misc = misc_ref[0]
    kr = pltpu.roll(misc, 64 - MISC_KPE, 1) * ck_ref[...] + pltpu.roll(misc, 64 - MISC_KPE_SW, 1) * sk_ref[...]
    cq = cq_ref[...]
    sq = sq_ref[...]
    ones_lane = lax.broadcasted_iota(jnp.int32, (vv.shape[0], LANES), 1) == MLA_V
    for h in range(MLA_HEADS):
        lo = LANES * h
        q_ref[0, h] = (qa[:, lo:lo + LANES] * cq + qb[:, lo:lo + LANES] * sq).astype(BF16)
        k_ref[0, h] = (kk[:, lo:lo + LANES] + kr).astype(BF16)
        v_ref[0, h] = jnp.where(ones_lane, 1.0, vv[:, lo:lo + LANES]).astype(BF16)


def _rope_tables(t):
    inv = ROPE_THETA ** (-jnp.arange(0, MLA_ROPE, 2, dtype=F32) / MLA_ROPE)
    ang = jnp.arange(t, dtype=F32)[:, None] * inv[None, :]
    cos, sin = jnp.cos(ang), jnp.sin(ang)
    z64 = jnp.zeros((t, 64), F32)
    z32 = jnp.zeros((t, 32), F32)
    one64 = jnp.ones((t, 64), F32)
    qscale = MLA_SCALE * math.log2(math.e)
    cq = jnp.concatenate([one64, cos, cos, z32], axis=1) * qscale
    sq = jnp.concatenate([z64, -sin, sin, z32], axis=1) * qscale
    ck = jnp.concatenate([z64, cos, cos, z32], axis=1)
    sk = jnp.concatenate([z64, -sin, sin, z32], axis=1)
    return cq, sq, ck, sk


def _mla_weights(w_qb, w_kvb):
    dq = MLA_NOPE + MLA_ROPE
    wq = w_qb.reshape(MLA_Q_RANK, MLA_HEADS, dq)
    nope, pe = wq[..., :MLA_NOPE], wq[..., MLA_NOPE:]
    pe_sw = jnp.concatenate([pe[..., MLA_ROPE // 2:], pe[..., :MLA_ROPE // 2]], axis=-1)
    zpad = jnp.zeros((MLA_Q_RANK, MLA_HEADS, LANES - dq), w_qb.dtype)
    wqa = jnp.concatenate([nope, pe, zpad], axis=-1).reshape(MLA_Q_RANK, MLA_HEADS * LANES)
    wqb = jnp.concatenate([jnp.zeros_like(nope), pe_sw, zpad], axis=-1).reshape(MLA_Q_RANK, MLA_HEADS * LANES)
    wkv = w_kvb.reshape(MLA_KV_RANK, MLA_HEADS, MLA_NOPE + MLA_V)
    wk = jnp.concatenate([wkv[..., :MLA_NOPE], jnp.zeros((MLA_KV_RANK, MLA_HEADS, LANES - MLA_NOPE), w_kvb.dtype)],
                         axis=-1).reshape(MLA_KV_RANK, MLA_HEADS * LANES)
    wv = jnp.concatenate([wkv[..., MLA_NOPE:], jnp.zeros((MLA_KV_RANK, MLA_HEADS, LANES - MLA_V), w_kvb.dtype)],
                         axis=-1).reshape(MLA_KV_RANK, MLA_HEADS * LANES)
    return wqa.astype(BF16), wqb.astype(BF16), wk.astype(BF16), wv.astype(BF16)


def _mla_prep(mla_in, misc, q_norm, kv_norm, w_qb, w_kvb, tables, *, tt):
    nb, t, _ = mla_in.shape
    wqa, wqb, wk, wv = _mla_weights(w_qb, w_kvb)
    row = lambda b, i: (b, i, 0)
    const2 = lambda b, i: (0, 0)
    tab = pl.BlockSpec((tt, LANES), lambda b, i: (i, 0))
    hrow = lambda b, i: (b, 0, i, 0)
    return pl.pallas_call(
        _mla_prep_kernel,
        grid=(nb, t // tt),
        in_specs=[
            pl.BlockSpec((1, tt, 384), row),
            pl.BlockSpec((1, tt, LANES), row),
            pl.BlockSpec((1, MLA_Q_RANK), const2),
            pl.BlockSpec((1, MLA_KV_RANK), const2),
            pl.BlockSpec((MLA_Q_RANK, MLA_HEADS * LANES), const2),
            pl.BlockSpec((MLA_Q_RANK, MLA_HEADS * LANES), const2),
            pl.BlockSpec((MLA_KV_RANK, MLA_HEADS * LANES), const2),
            pl.BlockSpec((MLA_KV_RANK, MLA_HEADS * LANES), const2),
            tab, tab, tab, tab,
        ],
        out_specs=[
            pl.BlockSpec((1, MLA_HEADS, tt, LANES), hrow),
            pl.BlockSpec((1, MLA_HEADS, tt, LANES), hrow),
            pl.BlockSpec((1, MLA_HEADS, tt, LANES), hrow),
        ],
        out_shape=[
            jax.ShapeDtypeStruct((nb, MLA_HEADS, t, LANES), BF16),
            jax.ShapeDtypeStruct((nb, MLA_HEADS, t, LANES), BF16),
            jax.ShapeDtypeStruct((nb, MLA_HEADS, t, LANES), BF16),
        ],
        compiler_params=_cparams(("arbitrary", "arbitrary")),
        name="mla_prep",
    )(mla_in, misc, q_norm.reshape(1, -1), kv_norm.reshape(1, -1), wqa, wqb, wk, wv, *tables)


def _flash_kernel(q_ref, k_ref, v_ref, o_ref, *, tq):
    i = pl.program_id(1)
    row = lax.broadcasted_iota(jnp.int32, (tq, tq), 0)
    col = lax.broadcasted_iota(jnp.int32, (tq, tq), 1)
    def step(j, carry, masked):
        start = pl.multiple_of(j * tq, tq)
        out = []
        for h in range(MLA_HEADS):
            m, acc = carry[h]
            kb = k_ref[0, h, pl.ds(start, tq), :]
            vb = v_ref[0, h, pl.ds(start, tq), :]
            s = _dot_nt(q_ref[0, h], kb)
            if masked:
                s = jnp.where(col <= row, s, -jnp.inf)
            m_new = jnp.maximum(m, jnp.max(s, axis=1, keepdims=True))
            p = jnp.exp2(s - m_new)
            acc = jnp.exp2(m - m_new) * acc + _dot(p.astype(BF16), vb)
            out.append((m_new, acc))
        return tuple(out)

    init = tuple((jnp.full((tq, 1), -jnp.inf, F32), jnp.zeros((tq, LANES), F32)) for _ in range(MLA_HEADS))
    carry = lax.fori_loop(0, i, functools.partial(step, masked=False), init)
    carry = step(i, carry, True)
    outs = [acc[:, 0:MLA_V] / acc[:, MLA_V:MLA_V + 1] for _, acc in carry]
    o_ref[0] = jnp.concatenate(outs, axis=-1).astype(BF16)


def _flash(q, k, v, *, tq):
    nb, nh, t, _ = q.shape
    return pl.pallas_call(
        functools.partial(_flash_kernel, tq=tq),
        grid=(nb, t // tq),
        in_specs=[
            pl.BlockSpec((1, nh, tq, LANES), lambda b, i: (b, 0, i, 0)),
            pl.BlockSpec((1, nh, t, LANES), lambda b, i: (b, 0, 0, 0)),
            pl.BlockSpec((1, nh, t, LANES), lambda b, i: (b, 0, 0, 0)),
        ],
        out_specs=pl.BlockSpec((1, tq, nh * MLA_V), lambda b, i: (b, i, 0)),
        out_shape=jax.ShapeDtypeStruct((nb, t, nh * MLA_V), BF16),
        compiler_params=_cparams(("arbitrary", "arbitrary")),
        name="mla_flash",
    )(q, k, v)


def _residual_ln(x, y, gate, g, b):
    return _layer_norm(DN_ALPHA * x + (1.0 + gate) * y, g, b)


def _outproj_kernel(*refs, route, tm):
    if route:
        (ys_ref, ym_ref, yg_ref, x_ref, mod1_ref, mod2_ref, w_ref, g_ref, b_ref, wr_ref,
         xo_ref, h_ref, rt_ref, cnt_ref, run_ref) = refs
    else:
        ys_ref, ym_ref, yg_ref, x_ref, mod1_ref, mod2_ref, w_ref, g_ref, b_ref, xo_ref, h_ref = refs
    y = (_dot(ys_ref[0], w_ref[0:512, :]) + _dot(ym_ref[0], w_ref[512:768, :])
         + _dot(yg_ref[0], w_ref[768:1024, :]))
    gate = mod1_ref[0][:, 2 * D_MODEL:3 * D_MODEL]
    xn = _residual_ln(x_ref[0], y, gate, g_ref[...], b_ref[...])
    xo_ref[0] = xn
    m2 = mod2_ref[0]
    h = xn * (1.0 + m2[:, D_MODEL:2 * D_MODEL]) + m2[:, 0:D_MODEL]
    if not route:
        h_ref[0] = h.astype(BF16)
        return
    h_ref[0] = h

    @pl.when((pl.program_id(0) == 0) & (pl.program_id(1) == 0))
    def _():
        run_ref[...] = jnp.zeros(run_ref.shape, F32)

    lane = lax.broadcasted_iota(jnp.int32, (tm, LANES), 1)
    logits = jnp.where(lane < N_EXPERTS, _dot(h.astype(BF16), wr_ref[...]), -jnp.inf)
    v1 = jnp.max(logits, axis=1, keepdims=True)
    i1 = jnp.min(jnp.where(logits == v1, lane, LANES), axis=1, keepdims=True)
    rest = jnp.where(lane == i1, -jnp.inf, logits)
    v2 = jnp.max(rest, axis=1, keepdims=True)
    i2 = jnp.min(jnp.where(rest == v2, lane, LANES), axis=1, keepdims=True)
    e2 = jnp.exp(v2 - v1)
    g1 = 1.0 / (1.0 + e2)
    g2 = e2 / (1.0 + e2)
    hot = ((lane == i1) | (lane == i2)).astype(F32)
    r_i = lax.broadcasted_iota(jnp.int32, (tm, tm), 0)
    c_i = lax.broadcasted_iota(jnp.int32, (tm, tm), 1)
    below = (c_i < r_i).astype(BF16)
    rank = _dot(below, hot.astype(BF16)) + run_ref[...]
    r1 = jnp.sum(jnp.where(lane == i1, rank, 0.0), axis=1, keepdims=True)
    r2 = jnp.sum(jnp.where(lane == i2, rank, 0.0), axis=1, keepdims=True)
    run = run_ref[...] + jnp.sum(hot, axis=0, keepdims=True)
    run_ref[...] = run
    cnt_ref[...] = jnp.broadcast_to(run, cnt_ref.shape)
    rt = jnp.where(lane == 0, i1.astype(F32), 0.0)
    rt = jnp.where(lane == 1, i2.astype(F32), rt)
    rt = jnp.where(lane == 2, r1, rt)
    rt = jnp.where(lane == 3, r2, rt)
    rt = jnp.where(lane == 4, g1, rt)
    rt = jnp.where(lane == 5, g2, rt)
    rt_ref[0] = rt


def _outproj(y_ssd, y_mla, y_gm, x, mod1, mod2, w_out, ln_g, ln_b, w_router=None, *, tm):
    nb, t, _ = x.shape
    route = w_router is not None
    row = lambda b, i: (b, i, 0)
    const2 = lambda b, i: (0, 0)
    modspec = pl.BlockSpec((1, 1, 3 * D_MODEL), lambda b, i: (b, 0, 0))
    in_specs = [
        pl.BlockSpec((1, tm, 512), row), pl.BlockSpec((1, tm, 256), row), pl.BlockSpec((1, tm, 256), row),
        pl.BlockSpec((1, tm, D_MODEL), row), modspec, modspec,
        pl.BlockSpec((D_MODEL, D_MODEL), const2),
        pl.BlockSpec((1, D_MODEL), const2), pl.BlockSpec((1, D_MODEL), const2),
    ]
    args = [y_ssd, y_mla, y_gm, x, mod1, mod2, w_out.astype(BF16), ln_g.reshape(1, -1), ln_b.reshape(1, -1)]
    out_specs = [pl.BlockSpec((1, tm, D_MODEL), row), pl.BlockSpec((1, tm, D_MODEL), row)]
    out_shape = [jax.ShapeDtypeStruct((nb, t, D_MODEL), F32),
                 jax.ShapeDtypeStruct((nb, t, D_MODEL), F32 if route else BF16)]
    scratch = []
    if route:
        wr = jnp.pad(w_router, ((0, 0), (0, LANES - N_EXPERTS))).astype(BF16)
        in_specs.append(pl.BlockSpec((D_MODEL, LANES), const2))
        args.append(wr)
        out_specs += [pl.BlockSpec((1, tm, LANES), row), pl.BlockSpec((8, LANES), const2)]
        out_shape += [jax.ShapeDtypeStruct((nb, t, LANES), F32), jax.ShapeDtypeStruct((8, LANES), F32)]
        scratch = [pltpu.VMEM((1, LANES), F32)]
    return pl.pallas_call(
        functools.partial(_outproj_kernel, route=route, tm=tm),
        grid=(nb, t // tm),
        in_specs=in_specs, out_specs=out_specs, out_shape=out_shape, scratch_shapes=scratch,
        compiler_params=_cparams(("arbitrary", "arbitrary")),
        name="outproj_route" if route else "outproj",
    )(*args)


def _swiglu_acc(h, w1_ref, w3_ref, w2_ref, widx, side_work=None):
    acc = None
    for j in range(D_FF // FF_CHUNK):
        lo = j * FF_CHUNK
        if side_work is not None:
            side_work(j)
        a = _dot(h, w1_ref[widx + (slice(None), slice(lo, lo + FF_CHUNK))])
        g = _dot(h, w3_ref[widx + (slice(None), slice(lo, lo + FF_CHUNK))])
        p = (_silu(a) * g).astype(BF16)
        part = _dot(p, w2_ref[widx + (slice(lo, lo + FF_CHUNK), slice(None))])
        acc = part if acc is None else acc + part
    return acc


def _ffn_kernel(h_ref, x_ref, mod_ref, w1_ref, w3_ref, w2_ref, g_ref, b_ref, o_ref):
    y = _swiglu_acc(h_ref[0], w1_ref, w3_ref, w2_ref, ())
    gate = mod_ref[0][:, 2 * D_MODEL:3 * D_MODEL]
    o_ref[0] = _residual_ln(x_ref[0], y, gate, g_ref[...], b_ref[...])


def _ffn(h, x, mod, w1, w3, w2, ln_g, ln_b, *, tm):
    nb, t, _ = x.shape
    row = lambda b, i: (b, i, 0)
    const2 = lambda b, i: (0, 0)
    wspec = lambda shape: pl.BlockSpec(shape, const2, pipeline_mode=pl.Buffered(1))
    return pl.pallas_call(
        _ffn_kernel,
        grid=(nb, t // tm),
        in_specs=[
            pl.BlockSpec((1, tm, D_MODEL), row), pl.BlockSpec((1, tm, D_MODEL), row),
            pl.BlockSpec((1, 1, 3 * D_MODEL), lambda b, i: (b, 0, 0)),
            wspec((D_MODEL, D_FF)), wspec((D_MODEL, D_FF)), wspec((D_FF, D_MODEL)),
            pl.BlockSpec((1, D_MODEL), const2), pl.BlockSpec((1, D_MODEL), const2),
        ],
        out_specs=pl.BlockSpec((1, tm, D_MODEL), row),
        out_shape=jax.ShapeDtypeStruct((nb, t, D_MODEL), F32),
        compiler_params=_cparams(("arbitrary", "arbitrary")),
        name="ffn_dense",
    )(h, x, mod, w1.astype(BF16), w3.astype(BF16), w2.astype(BF16), ln_g.reshape(1, -1), ln_b.reshape(1, -1))


def _invert_kernel(slot_ref, init_ref, inv_ref, sem, *, n_pairs):
    cp = pltpu.make_async_copy(init_ref, inv_ref, sem)
    cp.start()
    cp.wait()

    def body(i, c):
        inv_ref[slot_ref[i]] = i
        return c

    lax.fori_loop(0, n_pairs, body, 0, unroll=8)


def _moe_invert(slots, inv_init):
    return pl.pallas_call(
        functools.partial(_invert_kernel, n_pairs=slots.shape[0]),
        grid_spec=pltpu.PrefetchScalarGridSpec(
            num_scalar_prefetch=1,
            grid=(1,),
            in_specs=[pl.BlockSpec(memory_space=pl.ANY)],
            out_specs=pl.BlockSpec(memory_space=pltpu.SMEM),
            scratch_shapes=[pltpu.SemaphoreType.DMA(())],
        ),
        out_shape=jax.ShapeDtypeStruct(inv_init.shape, jnp.int32),
        compiler_params=_cparams(("arbitrary",)),
        name="moe_invert",
    )(slots, inv_init)


def _moe_ffn_kernel(te_ref, inv_ref, h_hbm, w1_ref, w3_ref, w2_ref, y_hbm, xbuf, obuf, gsem, ssem, *, tg, n_tok):
    del te_ref
    j = pl.program_id(0)
    last = pl.num_programs(0) - 1
    slot = lax.rem(j, 2)
    other = 1 - slot

    def gather_row(tile, r, sl):
        tok = inv_ref[(tile + 1) * tg + r] & (n_tok - 1)
        return pltpu.make_async_copy(h_hbm.at[pl.ds(tok, 1), :], xbuf.at[sl, pl.ds(r, 1), :], gsem.at[sl])

    def scatter_row(tile, r, sl):
        dst = inv_ref[(tile + 1) * tg + r]
        return pltpu.make_async_copy(obuf.at[sl, pl.ds(r, 1), :], y_hbm.at[pl.ds(dst, 1), :], ssem.at[sl])

    def wait_gather(sl):
        pltpu.make_async_copy(h_hbm.at[pl.ds(0, tg), :], xbuf.at[sl], gsem.at[sl]).wait()

    def wait_scatter(sl):
        pltpu.make_async_copy(obuf.at[sl], y_hbm.at[pl.ds(0, tg), :], ssem.at[sl]).wait()

    @pl.when(j == 0)
    def _():
        obuf[1] = jnp.zeros((tg, D_MODEL), F32)

        def first(r, c):
            gather_row(0, r, 0).start()
            return c

        lax.fori_loop(0, tg, first, 0)

    @pl.when(j > 0)
    def _():
        wait_scatter(slot)

    wait_gather(slot)
    x = xbuf[slot].astype(BF16)
    nxt = jnp.minimum(j + 1, last)
    n_chunks = D_FF // FF_CHUNK
    per = -(-tg // n_chunks)

    def side_work(c):
        for r in range(c * per, min((c + 1) * per, tg)):
            gather_row(nxt, r, other).start()
            scatter_row(j - 1, r, other).start()

    obuf[slot] = _swiglu_acc(x, w1_ref, w3_ref, w2_ref, (0,), side_work)

    @pl.when(j == last)
    def _():
        wait_gather(other)
        wait_scatter(other)

        def final(r, c):
            scatter_row(j, r, slot).start()
            return c

        lax.fori_loop(0, tg, final, 0)
        wait_scatter(slot)


def _moe_ffn(tile_expert, inv, h, w1, w3, w2, *, tg, n_out):
    n_tiles = tile_expert.shape[0]
    wmap = lambda j, te, iv: (te[j], 0, 0)
    return pl.pallas_call(
        functools.partial(_moe_ffn_kernel, tg=tg, n_tok=h.shape[0]),
        grid_spec=pltpu.PrefetchScalarGridSpec(
            num_scalar_prefetch=2,
            grid=(n_tiles,),
            in_specs=[
                pl.BlockSpec(memory_space=pl.ANY),
                pl.BlockSpec((1, D_MODEL, D_FF), wmap),
                pl.BlockSpec((1, D_MODEL, D_FF), wmap),
                pl.BlockSpec((1, D_FF, D_MODEL), wmap),
            ],
            out_specs=pl.BlockSpec(memory_space=pl.ANY),
            scratch_shapes=[
                pltpu.VMEM((2, tg, D_MODEL), F32), pltpu.VMEM((2, tg, D_MODEL), F32),
                pltpu.SemaphoreType.DMA((2,)), pltpu.SemaphoreType.DMA((2,)),
            ],
        ),
        out_shape=jax.ShapeDtypeStruct((n_out, D_MODEL), F32),
        compiler_params=_cparams(("arbitrary",)),
        name="moe_ffn",
    )(tile_expert, inv, h, w1.astype(BF16), w3.astype(BF16), w2.astype(BF16))


def _combine_kernel(y1_ref, y2_ref, rt_ref, x_ref, mod_ref, g_ref, b_ref, o_ref):
    rt = rt_ref[0]
    y = rt[:, 4:5] * y1_ref[...] + rt[:, 5:6] * y2_ref[...]
    gate = mod_ref[0][:, 2 * D_MODEL:3 * D_MODEL]
    o_ref[0] = _residual_ln(x_ref[0], y, gate, g_ref[...], b_ref[...])


def _combine(y2, route, x, mod, ln_g, ln_b, *, tm):
    nb, t, _ = x.shape
    row = lambda b, i: (b, i, 0)
    const2 = lambda b, i: (0, 0)
    nt = t // tm
    return pl.pallas_call(
        _combine_kernel,
        grid=(nb, nt),
        in_specs=[
            pl.BlockSpec((tm, D_MODEL), lambda b, i: (b * nt + i, 0)),
            pl.BlockSpec((tm, D_MODEL), lambda b, i: (nb * nt + b * nt + i, 0)),
            pl.BlockSpec((1, tm, LANES), row),
            pl.BlockSpec((1, tm, D_MODEL), row),
            pl.BlockSpec((1, 1, 3 * D_MODEL), lambda b, i: (b, 0, 0)),
            pl.BlockSpec((1, D_MODEL), const2), pl.BlockSpec((1, D_MODEL), const2),
        ],
        out_specs=pl.BlockSpec((1, tm, D_MODEL), row),
        out_shape=jax.ShapeDtypeStruct((nb, t, D_MODEL), F32),
        compiler_params=_cparams(("arbitrary", "arbitrary")),
        name="moe_combine",
    )(y2, y2, route, x, mod, ln_g.reshape(1, -1), ln_b.reshape(1, -1))


def _moe_plan(route, counts, *, tg, n_tiles):
    n_pairs = 2 * route.shape[0] * route.shape[1]
    cnt = counts[0, :N_EXPERTS].astype(jnp.int32)
    tiles = (cnt + tg - 1) // tg
    tile_end = jnp.cumsum(tiles)
    start = (tile_end - tiles) * tg
    r = route.reshape(-1, LANES)
    e = r[:, 0:2].astype(jnp.int32)
    assert (n_pairs // 2) & (n_pairs // 2 - 1) == 0, "token count must be a power of two (pair ids are masked)"
    slots = jnp.transpose(start[e] + r[:, 2:4].astype(jnp.int32)).reshape(-1) + tg
    n_active = tile_end[-1]
    j = jnp.minimum(jnp.arange(n_tiles, dtype=jnp.int32), n_active - 1)
    tile_expert = jnp.sum((j[:, None] >= tile_end[None, :]).astype(jnp.int32), axis=1)
    s = jnp.arange(n_tiles * tg, dtype=jnp.int32)
    es = jnp.minimum(jnp.sum((s[:, None] >= (tile_end * tg)[None, :]).astype(jnp.int32), axis=1), N_EXPERTS - 1)
    pad_before = jnp.cumsum(tiles * tg - cnt) - (tiles * tg - cnt)
    in_group = s - start[es]
    pad_id = jnp.where(s >= n_active * tg, s - n_pairs, pad_before[es] + in_group - cnt[es])
    n_spare = n_tiles * tg - n_pairs
    body = n_pairs + jnp.clip(pad_id, 0, n_spare - 1)
    head = n_pairs + n_spare + jnp.arange(tg, dtype=jnp.int32)
    return slots, tile_expert, jnp.concatenate([head, body]), n_pairs + n_spare + tg


def _w_in_padded(w_in):
    z, xs, bm, cm = (0, 512), (512, 1024), (1024, 1152), (1152, 1280)
    dt, ql, kvl, kpe, gu, gv = (1280, 1288), (1288, 1544), (1544, 1672), (1672, 1704), (1704, 1960), (1960, 2216)
    seg = lambda ab: w_in[:, ab[0]:ab[1]]
    k_pe = seg(kpe)
    k_sw = jnp.concatenate([k_pe[:, MLA_ROPE // 2:], k_pe[:, :MLA_ROPE // 2]], axis=1)
    pad = jnp.zeros((D_MODEL, LANES - 2 * MLA_ROPE - SSD_HEADS), w_in.dtype)
    cols = [seg(z), seg(xs), seg(bm), seg(cm), seg(ql), seg(kvl), seg(gu), seg(gv), k_pe, k_sw, seg(dt), pad]
    return jnp.concatenate(cols, axis=1).astype(BF16)


def _forward(x, c, ln0_g, ln0_b, ada_w, ada_b, post_ln_g, post_ln_b, w_in, ssd_conv_w, ssd_conv_b,
             ssd_dt_bias, ssd_a_log, ssd_d, ssd_norm_w, mla_q_norm, mla_w_qb, mla_kv_norm, mla_w_kvb,
             gm_ln_g, gm_ln_b, gm_w_s, gm_b_s, w_out, ffn_w1, ffn_w3, ffn_w2,
             moe_router, moe_w1, moe_w3, moe_w2, *, tm, tq, tg, tc):
    nb, t, _ = x.shape
    mods = _modulations(c, ada_w, ada_b).reshape(2 * DEPTH, nb, 1, 3 * D_MODEL)
    tables = _rope_tables(t)
    for layer in range(DEPTH):
        mod_mix, mod_ffn = mods[2 * layer], mods[2 * layer + 1]
        w_pad = _w_in_padded(w_in[layer])
        if layer == 0:
            x, p_ssd, p_mla, p_gm, p_misc = _inproj(x, mod_mix, w_pad, ln0_g, ln0_b, tm=tm)
        else:
            p_ssd, p_mla, p_gm, p_misc = _inproj(x, mod_mix, w_pad, tm=tm)
        y_ssd = _ssd(p_ssd, p_misc, ssd_conv_w[layer], ssd_conv_b[layer], ssd_dt_bias[layer],
                     ssd_a_log[layer], ssd_d[layer], ssd_norm_w[layer], ts=tm)
        q, k, v = _mla_prep(p_mla, p_misc, mla_q_norm[layer], mla_kv_norm[layer], mla_w_qb[layer],
                            mla_w_kvb[layer], tables, tt=tm)
        y_mla = _flash(q, k, v, tq=tq)
        y_gm = _gmlp(p_gm, gm_ln_g[layer], gm_ln_b[layer], gm_w_s[layer], gm_b_s[layer], ts=tm)
        i = layer // 2
        if layer % 2 == 0:
            x, h = _outproj(y_ssd, y_mla, y_gm, x, mod_mix, mod_ffn, w_out[layer],
                            post_ln_g[layer, 0], post_ln_b[layer, 0], tm=tm)
            x = _ffn(h, x, mod_ffn, ffn_w1[i], ffn_w3[i], ffn_w2[i], post_ln_g[layer, 1], post_ln_b[layer, 1], tm=tm)
        else:
            x, h, route, counts = _outproj(y_ssd, y_mla, y_gm, x, mod_mix, mod_ffn, w_out[layer],
                                           post_ln_g[layer, 0], post_ln_b[layer, 0], moe_router[i], tm=tm)
            n_tiles = (2 * nb * t) // tg + N_EXPERTS
            slots, tile_expert, inv_init, n_out = _moe_plan(route, counts, tg=tg, n_tiles=n_tiles)
            inv = _moe_invert(slots, inv_init)
            y2 = _moe_ffn(tile_expert, inv, h.reshape(nb * t, D_MODEL), moe_w1[i], moe_w3[i], moe_w2[i],
                          tg=tg, n_out=n_out)
            x = _combine(y2, route, x, mod_ffn, post_ln_g[layer, 1], post_ln_b[layer, 1], tm=tc)
    return x


def kernel(x, c, ln0_g, ln0_b, ada_w, ada_b, post_ln_g, post_ln_b, w_in, ssd_conv_w, ssd_conv_b, ssd_dt_bias, ssd_a_log, ssd_d, ssd_norm_w, mla_q_norm, mla_w_qb, mla_kv_norm, mla_w_kvb, gm_ln_g, gm_ln_b, gm_w_s, gm_b_s, w_out, ffn_w1, ffn_w3, ffn_w2, moe_router, moe_w1, moe_w3, moe_w2):
    t = x.shape[1]
    tm = min(512, t)
    return _forward(x, c, ln0_g, ln0_b, ada_w, ada_b, post_ln_g, post_ln_b, w_in, ssd_conv_w, ssd_conv_b,
                    ssd_dt_bias, ssd_a_log, ssd_d, ssd_norm_w, mla_q_norm, mla_w_qb, mla_kv_norm, mla_w_kvb,
                    gm_ln_g, gm_ln_b, gm_w_s, gm_b_s, w_out, ffn_w1, ffn_w3, ffn_w2,
                    moe_router, moe_w1, moe_w3, moe_w2, tm=tm, tq=min(1024, t), tg=tm, tc=tm)
```

```python
import functools
import math

import numpy as np
import jax
import jax.numpy as jnp
from jax import lax
from jax.experimental import pallas as pl
from jax.experimental.pallas import tpu as pltpu

F32 = jnp.float32
BF16 = jnp.bfloat16

D_MODEL = 1024
DEPTH = 2
SSD_INNER = 512
SSD_HEAD_DIM = 64
SSD_HEADS = 8
SSD_GROUPS = 2
SSD_STATE = 64
SSD_CONV = 4
CHUNK = 128
SSD_CONV_DIM = 768
MLA_HEADS = 4
MLA_Q_RANK = 256
MLA_KV_RANK = 128
MLA_NOPE = 64
MLA_ROPE = 32
MLA_V = 64
MLA_SCALE = (MLA_NOPE + MLA_ROPE) ** -0.5
ROPE_THETA = 10000.0
GM_GROUPS = 4
GM_GROUP_DIM = 64
GM_WIDTH = 256
D_FF = 2816
N_EXPERTS = 8
DN_ALPHA = (2 * DEPTH) ** 0.25
LN_EPS = 1e-5
RMS_EPS = 1e-6

LANES = 128
FF_CHUNK = 256
VMEM_LIMIT = 56 * 1024 * 1024

MISC_KPE = 0
MISC_KPE_SW = 32
MISC_DT = 64
D_IN_PAD = 1280 + 384 + 512 + 128


def _cparams(sem):
    return pltpu.CompilerParams(dimension_semantics=sem, vmem_limit_bytes=VMEM_LIMIT)


def _silu(x):
    return x * jax.nn.sigmoid(x)


def _gelu(x):
    return 0.5 * x * (1.0 + lax.erf(x * (2.0 ** -0.5)))


def _softplus(x):
    return jnp.maximum(x, 0.0) + jnp.log1p(jnp.exp(-jnp.abs(x)))


def _layer_norm(x, g, b):
    mu = jnp.mean(x, axis=-1, keepdims=True)
    xc = x - mu
    var = jnp.mean(xc * xc, axis=-1, keepdims=True)
    return xc * lax.rsqrt(var + LN_EPS) * g + b


def _rms(x):
    return x * lax.rsqrt(jnp.mean(x * x, axis=-1, keepdims=True) + RMS_EPS)


def _dot(a, b):
    return jnp.dot(a, b, preferred_element_type=F32)


def _dot_nt(a, b):
    return lax.dot_general(a, b, (((1,), (1,)), ((), ())), preferred_element_type=F32)


def _dot_tn(a, b):
    return lax.dot_general(a, b, (((0,), (0,)), ((), ())), preferred_element_type=F32)


def _mod_kernel(c_ref, w_ref, b_ref, o_ref):
    s = _silu(c_ref[...]).astype(BF16)
    o_ref[0] = _dot(s, w_ref[0].astype(BF16)) + b_ref[0]


def _modulations(c, ada_w, ada_b):
    nb = c.shape[0]
    n = ada_w.shape[0] * ada_w.shape[1]
    w = ada_w.reshape(n, D_MODEL, 3 * D_MODEL)
    b = ada_b.reshape(n, 1, 3 * D_MODEL)
    tn = 1024
    return pl.pallas_call(
        _mod_kernel,
        grid=(n, 3 * D_MODEL // tn),
        in_specs=[
            pl.BlockSpec((nb, D_MODEL), lambda j, k: (0, 0)),
            pl.BlockSpec((1, D_MODEL, tn), lambda j, k: (j, 0, k)),
            pl.BlockSpec((1, 1, tn), lambda j, k: (j, 0, k)),
        ],
        out_specs=pl.BlockSpec((1, nb, tn), lambda j, k: (j, 0, k)),
        out_shape=jax.ShapeDtypeStruct((n, nb, 3 * D_MODEL), F32),
        compiler_params=_cparams(("arbitrary", "arbitrary")),
        name="adaln_mod",
    )(c, w, b)


def _inproj_kernel(*refs, do_ln):
    if do_ln:
        x_ref, mod_ref, g_ref, b_ref, w_ref, xo_ref, ssd_ref, mla_ref, gm_ref, misc_ref = refs
    else:
        x_ref, mod_ref, w_ref, ssd_ref, mla_ref, gm_ref, misc_ref = refs
    x = x_ref[0]
    if do_ln:
        x = _layer_norm(x, g_ref[...], b_ref[...])
        xo_ref[0] = x
    m = mod_ref[0]
    h = x * (1.0 + m[:, D_MODEL:2 * D_MODEL]) + m[:, 0:D_MODEL]
    p = _dot(h.astype(BF16), w_ref[...])
    ssd_ref[0] = p[:, 0:1280].astype(BF16)
    mla_ref[0] = p[:, 1280:1664].astype(BF16)
    gm_ref[0] = p[:, 1664:2176].astype(BF16)
    misc_ref[0] = p[:, 2176:2304]


def _inproj(x, mod, w_pad, ln_g=None, ln_b=None, *, tm):
    nb, t, _ = x.shape
    do_ln = ln_g is not None
    row = lambda b, i: (b, i, 0)
    const2 = lambda b, i: (0, 0)
    in_specs = [pl.BlockSpec((1, tm, D_MODEL), row), pl.BlockSpec((1, 1, 3 * D_MODEL), lambda b, i: (b, 0, 0))]
    args = [x, mod]
    if do_ln:
        in_specs += [pl.BlockSpec((1, D_MODEL), const2), pl.BlockSpec((1, D_MODEL), const2)]
        args += [ln_g.reshape(1, D_MODEL), ln_b.reshape(1, D_MODEL)]
    in_specs.append(pl.BlockSpec((D_MODEL, D_IN_PAD), const2))
    args.append(w_pad)
    out_specs = [pl.BlockSpec((1, tm, 1280), row), pl.BlockSpec((1, tm, 384), row),
                 pl.BlockSpec((1, tm, 512), row), pl.BlockSpec((1, tm, LANES), row)]
    out_shape = [jax.ShapeDtypeStruct((nb, t, 1280), BF16), jax.ShapeDtypeStruct((nb, t, 384), BF16),
                 jax.ShapeDtypeStruct((nb, t, 512), BF16), jax.ShapeDtypeStruct((nb, t, LANES), F32)]
    if do_ln:
        out_specs = [pl.BlockSpec((1, tm, D_MODEL), row)] + out_specs
        out_shape = [jax.ShapeDtypeStruct((nb, t, D_MODEL), F32)] + out_shape
    return pl.pallas_call(
        functools.partial(_inproj_kernel, do_ln=do_ln),
        grid=(nb, t // tm),
        in_specs=in_specs, out_specs=out_specs, out_shape=out_shape,
        compiler_params=_cparams(("arbitrary", "arbitrary")),
        name="inproj_ln" if do_ln else "inproj",
    )(*args)


def _ssd_kernel(ssd_ref, misc_ref, cw_ref, cb_ref, dtb_ref, arow_ref, drow_ref, nw_ref, o_ref,
                xpad_ref, st_ref, y_ref, *, ts):
    L = CHUNK

    @pl.when(pl.program_id(1) == 0)
    def _():
        xpad_ref[0:8, :] = jnp.zeros((8, SSD_CONV_DIM), F32)
        st_ref[...] = jnp.zeros(st_ref.shape, F32)

    blk = ssd_ref[0]
    xpad_ref[8:8 + ts, :] = blk[:, SSD_INNER:SSD_INNER + SSD_CONV_DIM].astype(F32)
    cw = cw_ref[...]
    acc = cb_ref[...] + cw[3:4, :] * xpad_ref[8:8 + ts, :]
    for k in range(SSD_CONV - 1):
        acc = acc + cw[k:k + 1, :] * xpad_ref[pl.ds(5 + k, ts), :]
    xpad_ref[0:8, :] = xpad_ref[ts:ts + 8, :]
    xbc = _silu(acc)

    dtv_all = _softplus(misc_ref[0] + dtb_ref[...])
    a_all = dtv_all * arow_ref[...]
    row = lax.broadcasted_iota(jnp.int32, (L, LANES), 0)
    col = lax.broadcasted_iota(jnp.int32, (L, LANES), 1)
    causal = row >= col
    lo_half = col < SSD_HEAD_DIM
    lo_half_row = lo_half[0:1, :]
    heads_per_group = SSD_HEADS // SSD_GROUPS

    for c in range(ts // L):
        r0 = c * L
        xs = xbc[r0:r0 + L, 0:SSD_INNER]
        bm = xbc[r0:r0 + L, SSD_INNER:SSD_INNER + 128]
        cm = xbc[r0:r0 + L, SSD_INNER + 128:SSD_INNER + 256]
        dtv = dtv_all[r0:r0 + L, :]
        acs = a_all[r0:r0 + L, :]
        s = 1
        while s < L:
            acs = acs + jnp.where(row >= s, pltpu.roll(acs, s, 0), 0.0)
            s *= 2
        acs_t = acs.T
        atot = acs[L - 1:L, :]
        e_out = jnp.exp(acs)
        e_end = jnp.exp(atot - acs)
        e_tot = jnp.exp(atot)

        for g in range(SSD_GROUPS):
            in_group = (col >= SSD_STATE * g) & (col < SSD_STATE * (g + 1))
            bg = jnp.where(in_group, bm, 0.0).astype(BF16)
            cg = jnp.where(in_group, cm, 0.0).astype(BF16)
            cb = _dot_nt(cg, bg)
            for q in range(heads_per_group // 2):
                pair = g * (heads_per_group // 2) + q
                l0 = MISC_DT + 2 * pair
                l1 = l0 + 1
                lanes = slice(LANES * pair, LANES * pair + LANES)

                def pair_cols(arr):
                    return jnp.where(lo_half, arr[:, l0:l0 + 1], arr[:, l1:l1 + 1])

                x_p = xs[:, lanes]
                xdt = x_p * pair_cols(dtv)
                xdt_b = xdt.astype(BF16)
                halves = []
                for ll in (l0, l1):
                    decay = jnp.exp(jnp.where(causal, acs[:, ll:ll + 1] - acs_t[ll:ll + 1, :], -jnp.inf))
                    halves.append(_dot((cb * decay).astype(BF16), xdt_b))
                y = jnp.where(lo_half, halves[0], halves[1])
                st = st_ref[pair]
                y = y + pair_cols(e_out) * _dot(cg, st.astype(BF16)) + x_p * drow_ref[:, lanes]
                wst = (xdt * pair_cols(e_end)).astype(BF16)
                tot = jnp.where(lo_half_row, e_tot[:, l0:l0 + 1], e_tot[:, l1:l1 + 1])
                st_ref[pair] = tot * st + _dot_tn(bg, wst)
                y_ref[r0:r0 + L, lanes] = y

    gy = y_ref[...] * _silu(blk[:, 0:SSD_INNER].astype(F32))
    half = SSD_INNER // SSD_GROUPS
    for g in range(SSD_GROUPS):
        seg = gy[:, half * g:half * g + half]
        o_ref[0, :, half * g:half * g + half] = (_rms(seg) * nw_ref[:, half * g:half * g + half]).astype(BF16)


def _ssd(ssd_in, misc, conv_w, conv_b, dt_bias, a_log, d_skip, norm_w, *, ts):
    nb, t, _ = ssd_in.shape
    lane_pad = (MISC_DT, LANES - MISC_DT - SSD_HEADS)
    dtb = jnp.pad(dt_bias.astype(F32), lane_pad).reshape(1, LANES)
    arow = jnp.pad(-jnp.exp(a_log.astype(F32)), lane_pad).reshape(1, LANES)
    drow = jnp.repeat(d_skip.astype(F32), SSD_HEAD_DIM).reshape(1, SSD_INNER)
    row = lambda b, i: (b, i, 0)
    const2 = lambda b, i: (0, 0)
    return pl.pallas_call(
        functools.partial(_ssd_kernel, ts=ts),
        grid=(nb, t // ts),
        in_specs=[
            pl.BlockSpec((1, ts, 1280), row),
            pl.BlockSpec((1, ts, LANES), row),
            pl.BlockSpec((SSD_CONV, SSD_CONV_DIM), const2),
            pl.BlockSpec((1, SSD_CONV_DIM), const2),
            pl.BlockSpec((1, LANES), const2),
            pl.BlockSpec((1, LANES), const2),
            pl.BlockSpec((1, SSD_INNER), const2),
            pl.BlockSpec((1, SSD_INNER), const2),
        ],
        out_specs=pl.BlockSpec((1, ts, SSD_INNER), row),
        out_shape=jax.ShapeDtypeStruct((nb, t, SSD_INNER), BF16),
        scratch_shapes=[
            pltpu.VMEM((ts + 8, SSD_CONV_DIM), F32),
            pltpu.VMEM((SSD_HEADS // 2, SSD_GROUPS * SSD_STATE, 2 * SSD_HEAD_DIM), F32),
            pltpu.VMEM((ts, SSD_INNER), F32),
        ],
        compiler_params=_cparams(("arbitrary", "arbitrary")),
        name="ssd_scan",
    )(ssd_in, misc, conv_w, conv_b.reshape(1, -1), dtb, arow, drow, norm_w.reshape(1, -1))


def _gmlp_kernel(gm_ref, lg_ref, lb_ref, ws_ref, bs_ref, o_ref, *, ts):
    blk = gm_ref[0].astype(F32)
    gu = _gelu(blk[:, 0:GM_WIDTH])
    gv = _layer_norm(_gelu(blk[:, GM_WIDTH:2 * GM_WIDTH]), lg_ref[...], lb_ref[...]).astype(BF16)
    row = lax.broadcasted_iota(jnp.int32, (CHUNK, CHUNK), 0)
    col = lax.broadcasted_iota(jnp.int32, (CHUNK, CHUNK), 1)
    ws = [jnp.where(row >= col, ws_ref[g], 0.0).astype(BF16) for g in range(GM_GROUPS)]
    group = lax.broadcasted_iota(jnp.int32, (CHUNK, GM_WIDTH), 1) // GM_GROUP_DIM
    for c in range(ts // CHUNK):
        r0 = c * CHUNK
        v = gv[r0:r0 + CHUNK, :]
        s = _dot(ws[GM_GROUPS - 1], v)
        for g in range(GM_GROUPS - 2, -1, -1):
            s = jnp.where(group == g, _dot(ws[g], v), s)
        o_ref[0, r0:r0 + CHUNK, :] = (gu[r0:r0 + CHUNK, :] * (s + bs_ref[...])).astype(BF16)


def _gmlp(gm_in, ln_g, ln_b, w_s, b_s, *, ts):
    nb, t, _ = gm_in.shape
    bs = jnp.repeat(jnp.transpose(b_s), GM_GROUP_DIM, axis=1)
    row = lambda b, i: (b, i, 0)
    return pl.pallas_call(
        functools.partial(_gmlp_kernel, ts=ts),
        grid=(nb, t // ts),
        in_specs=[
            pl.BlockSpec((1, ts, 2 * GM_WIDTH), row),
            pl.BlockSpec((1, GM_WIDTH), lambda b, i: (0, 0)),
            pl.BlockSpec((1, GM_WIDTH), lambda b, i: (0, 0)),
            pl.BlockSpec((GM_GROUPS, CHUNK, CHUNK), lambda b, i: (0, 0, 0)),
            pl.BlockSpec((CHUNK, GM_WIDTH), lambda b, i: (0, 0)),
        ],
        out_specs=pl.BlockSpec((1, ts, GM_WIDTH), row),
        out_shape=jax.ShapeDtypeStruct((nb, t, GM_WIDTH), BF16),
        compiler_params=_cparams(("arbitrary", "arbitrary")),
        name="gmlp_gate",
    )(gm_in, ln_g.reshape(1, -1), ln_b.reshape(1, -1), w_s, bs)


def _mla_prep_kernel(mla_ref, misc_ref, qn_ref, kvn_ref, wqa_ref, wqb_ref, wk_ref, wv_ref,
                     cq_ref, sq_ref, ck_ref, sk_ref, q_ref, k_ref, v_ref):
    m = mla_ref[0].astype(F32)
    qn = (_rms(m[:, 0:MLA_Q_RANK]) * qn_ref[...]).astype(BF16)
    kvn = (_rms(m[:, MLA_Q_RANK:MLA_Q_RANK + MLA_KV_RANK]) * kvn_ref[...]).astype(BF16)
    qa = _dot(qn, wqa_ref[...])
    qb = _dot(qn, wqb_ref[...])
    kk = _dot(kvn, wk_ref[...])
    vv = _dot(kvn, wv_ref[...])
    misc = misc_ref[0]
    kr = pltpu.roll(misc, 64 - MISC_KPE, 1) * ck_ref[...] + pltpu.roll(misc, 64 - MISC_KPE_SW, 1) * sk_ref[...]
    cq = cq_ref[...]
    sq = sq_ref[...]
    ones_lane = lax.broadcasted_iota(jnp.int32, (vv.shape[0], LANES), 1) == MLA_V
    for h in range(MLA_HEADS):
        lo = LANES * h
        q_ref[0, h] = (qa[:, lo:lo + LANES] * cq + qb[:, lo:lo + LANES] * sq).astype(BF16)
        k_ref[0, h] = (kk[:, lo:lo + LANES] + kr).astype(BF16)
        v_ref[0, h] = jnp.where(ones_lane, 1.0, vv[:, lo:lo + LANES]).astype(BF16)


def _rope_tables(t):
    inv = ROPE_THETA ** (-jnp.arange(0, MLA_ROPE, 2, dtype=F32) / MLA_ROPE)
    ang = jnp.arange(t, dtype=F32)[:, None] * inv[None, :]
    cos, sin = jnp.cos(ang), jnp.sin(ang)
    z64 = jnp.zeros((t, 64), F32)
    z32 = jnp.zeros((t, 32), F32)
    one64 = jnp.ones((t, 64), F32)
    qscale = MLA_SCALE * math.log2(math.e)
    cq = jnp.concatenate([one64, cos, cos, z32], axis=1) * qscale
    sq = jnp.concatenate([z64, -sin, sin, z32], axis=1) * qscale
    ck = jnp.concatenate([z64, cos, cos, z32], axis=1)
    sk = jnp.concatenate([z64, -sin, sin, z32], axis=1)
    return cq, sq, ck, sk


def _mla_weights(w_qb, w_kvb):
    dq = MLA_NOPE + MLA_ROPE
    wq = w_qb.reshape(MLA_Q_RANK, MLA_HEADS, dq)
    nope, pe = wq[..., :MLA_NOPE], wq[..., MLA_NOPE:]
    pe_sw = jnp.concatenate([pe[..., MLA_ROPE // 2:], pe[..., :MLA_ROPE // 2]], axis=-1)
    zpad = jnp.zeros((MLA_Q_RANK, MLA_HEADS, LANES - dq), w_qb.dtype)
    wqa = jnp.concatenate([nope, pe, zpad], axis=-1).reshape(MLA_Q_RANK, MLA_HEADS * LANES)
    wqb = jnp.concatenate([jnp.zeros_like(nope), pe_sw, zpad], axis=-1).reshape(MLA_Q_RANK, MLA_HEADS * LANES)
    wkv = w_kvb.reshape(MLA_KV_RANK, MLA_HEADS, MLA_NOPE + MLA_V)
    wk = jnp.concatenate([wkv[..., :MLA_NOPE], jnp.zeros((MLA_KV_RANK, MLA_HEADS, LANES - MLA_NOPE), w_kvb.dtype)],
                         axis=-1).reshape(MLA_KV_RANK, MLA_HEADS * LANES)
    wv = jnp.concatenate([wkv[..., MLA_NOPE:], jnp.zeros((MLA_KV_RANK, MLA_HEADS, LANES - MLA_V), w_kvb.dtype)],
                         axis=-1).reshape(MLA_KV_RANK, MLA_HEADS * LANES)
    return wqa.astype(BF16), wqb.astype(BF16), wk.astype(BF16), wv.astype(BF16)


def _mla_prep(mla_in, misc, q_norm, kv_norm, w_qb, w_kvb, tables, *, tt):
    nb, t, _ = mla_in.shape
    wqa, wqb, wk, wv = _mla_weights(w_qb, w_kvb)
    row = lambda b, i: (b, i, 0)
    const2 = lambda b, i: (0, 0)
    tab = pl.BlockSpec((tt, LANES), lambda b, i: (i, 0))
    hrow = lambda b, i: (b, 0, i, 0)
    return pl.pallas_call(
        _mla_prep_kernel,
        grid=(nb, t // tt),
        in_specs=[
            pl.BlockSpec((1, tt, 384), row),
            pl.BlockSpec((1, tt, LANES), row),
            pl.BlockSpec((1, MLA_Q_RANK), const2),
            pl.BlockSpec((1, MLA_KV_RANK), const2),
            pl.BlockSpec((MLA_Q_RANK, MLA_HEADS * LANES), const2),
            pl.BlockSpec((MLA_Q_RANK, MLA_HEADS * LANES), const2),
            pl.BlockSpec((MLA_KV_RANK, MLA_HEADS * LANES), const2),
            pl.BlockSpec((MLA_KV_RANK, MLA_HEADS * LANES), const2),
            tab, tab, tab, tab,
        ],
        out_specs=[
            pl.BlockSpec((1, MLA_HEADS, tt, LANES), hrow),
            pl.BlockSpec((1, MLA_HEADS, tt, LANES), hrow),
            pl.BlockSpec((1, MLA_HEADS, tt, LANES), hrow),
        ],
        out_shape=[
            jax.ShapeDtypeStruct((nb, MLA_HEADS, t, LANES), BF16),
            jax.ShapeDtypeStruct((nb, MLA_HEADS, t, LANES), BF16),
            jax.ShapeDtypeStruct((nb, MLA_HEADS, t, LANES), BF16),
        ],
        compiler_params=_cparams(("arbitrary", "arbitrary")),
        name="mla_prep",
    )(mla_in, misc, q_norm.reshape(1, -1), kv_norm.reshape(1, -1), wqa, wqb, wk, wv, *tables)


def _flash_kernel(q_ref, k_ref, v_ref, o_ref, *, tq):
    i = pl.program_id(1)
    tri = lax.broadcasted_iota(jnp.int32, (tq, tq), 1) <= lax.broadcasted_iota(jnp.int32, (tq, tq), 0)
    def attend(q, kb, vb, m, acc, mask):
        s = _dot_nt(q, kb)
        if mask is not None:
            s = jnp.where(mask, s, -jnp.inf)
        m_new = jnp.maximum(m, jnp.max(s, axis=1, keepdims=True))
        p = jnp.exp2(s - m_new)
        return m_new, jnp.exp2(m - m_new) * acc + _dot(p.astype(BF16), vb)

    def step(j, carry, mask):
        start = pl.multiple_of(j * tq, tq)
        return tuple(attend(q_ref[0, h], k_ref[0, h, pl.ds(start, tq), :], v_ref[0, h, pl.ds(start, tq), :],
                            carry[h][0], carry[h][1], mask) for h in range(MLA_HEADS))

    init = tuple((jnp.full((tq, 1), -jnp.inf, F32), jnp.zeros((tq, LANES), F32)) for _ in range(MLA_HEADS))
    carry = lax.fori_loop(0, i, functools.partial(step, mask=None), init)
    carry = step(i, carry, tri)
    outs = [acc[:, 0:MLA_V] / acc[:, MLA_V:MLA_V + 1] for _, acc in carry]
    o_ref[0] = jnp.concatenate(outs, axis=-1).astype(BF16)


def _flash(q, k, v, *, tq):
    nb, nh, t, _ = q.shape
    return pl.pallas_call(
        functools.partial(_flash_kernel, tq=tq),
        grid=(nb, t // tq),
        in_specs=[
            pl.BlockSpec((1, nh, tq, LANES), lambda b, i: (b, 0, i, 0)),
            pl.BlockSpec((1, nh, t, LANES), lambda b, i: (b, 0, 0, 0)),
            pl.BlockSpec((1, nh, t, LANES), lambda b, i: (b, 0, 0, 0)),
        ],
        out_specs=pl.BlockSpec((1, tq, nh * MLA_V), lambda b, i: (b, i, 0)),
        out_shape=jax.ShapeDtypeStruct((nb, t, nh * MLA_V), BF16),
        compiler_params=_cparams(("arbitrary", "arbitrary")),
        name="mla_flash",
    )(q, k, v)


def _residual_ln(x, y, gate, g, b):
    return _layer_norm(DN_ALPHA * x + (1.0 + gate) * y, g, b)


def _outproj_kernel(*refs, route, tm):
    if route:
        (ys_ref, ym_ref, yg_ref, x_ref, mod1_ref, mod2_ref, w_ref, g_ref, b_ref, wr_ref,
         xo_ref, h_ref, rt_ref, cnt_ref, run_ref) = refs
    else:
        ys_ref, ym_ref, yg_ref, x_ref, mod1_ref, mod2_ref, w_ref, g_ref, b_ref, xo_ref, h_ref = refs
    y = (_dot(ys_ref[0], w_ref[0:512, :]) + _dot(ym_ref[0], w_ref[512:768, :])
         + _dot(yg_ref[0], w_ref[768:1024, :]))
    gate = mod1_ref[0][:, 2 * D_MODEL:3 * D_MODEL]
    xn = _residual_ln(x_ref[0], y, gate, g_ref[...], b_ref[...])
    xo_ref[0] = xn
    m2 = mod2_ref[0]
    h = xn * (1.0 + m2[:, D_MODEL:2 * D_MODEL]) + m2[:, 0:D_MODEL]
    if not route:
        h_ref[0] = h.astype(BF16)
        return
    h_ref[0] = h

    @pl.when((pl.program_id(0) == 0) & (pl.program_id(1) == 0))
    def _():
        run_ref[...] = jnp.zeros(run_ref.shape, F32)

    lane = lax.broadcasted_iota(jnp.int32, (tm, LANES), 1).astype(F32)
    logits = jnp.where(lane < N_EXPERTS, _dot(h.astype(BF16), wr_ref[...]), -jnp.inf)
    v1 = jnp.max(logits, axis=1, keepdims=True)
    i1 = jnp.min(jnp.where(logits == v1, lane, float(LANES)), axis=1, keepdims=True)
    rest = jnp.where(lane == i1, -jnp.inf, logits)
    v2 = jnp.max(rest, axis=1, keepdims=True)
    i2 = jnp.min(jnp.where(rest == v2, lane, float(LANES)), axis=1, keepdims=True)
    e2 = jnp.exp(v2 - v1)
    g1 = 1.0 / (1.0 + e2)
    g2 = e2 / (1.0 + e2)
    hot = ((lane == i1) | (lane == i2)).astype(F32)
    r_i = lax.broadcasted_iota(jnp.int32, (tm, tm), 0)
    c_i = lax.broadcasted_iota(jnp.int32, (tm, tm), 1)
    below = (c_i < r_i).astype(BF16)
    rank = _dot(below, hot.astype(BF16)) + run_ref[...]
    r1 = jnp.sum(jnp.where(lane == i1, rank, 0.0), axis=1, keepdims=True)
    r2 = jnp.sum(jnp.where(lane == i2, rank, 0.0), axis=1, keepdims=True)
    run = run_ref[...] + jnp.sum(hot, axis=0, keepdims=True)
    run_ref[...] = run
    cnt_ref[...] = jnp.broadcast_to(run, cnt_ref.shape)
    rt = jnp.where(lane == 0, i1, 0.0)
    rt = jnp.where(lane == 1, i2, rt)
    rt = jnp.where(lane == 2, r1, rt)
    rt = jnp.where(lane == 3, r2, rt)
    rt = jnp.where(lane == 4, g1, rt)
    rt = jnp.where(lane == 5, g2, rt)
    rt_ref[0] = rt


def _outproj(y_ssd, y_mla, y_gm, x, mod1, mod2, w_out, ln_g, ln_b, w_router=None, *, tm):
    nb, t, _ = x.shape
    route = w_router is not None
    row = lambda b, i: (b, i, 0)
    const2 = lambda b, i: (0, 0)
    modspec = pl.BlockSpec((1, 1, 3 * D_MODEL), lambda b, i: (b, 0, 0))
    in_specs = [
        pl.BlockSpec((1, tm, 512), row), pl.BlockSpec((1, tm, 256), row), pl.BlockSpec((1, tm, 256), row),
        pl.BlockSpec((1, tm, D_MODEL), row), modspec, modspec,
        pl.BlockSpec((D_MODEL, D_MODEL), const2),
        pl.BlockSpec((1, D_MODEL), const2), pl.BlockSpec((1, D_MODEL), const2),
    ]
    args = [y_ssd, y_mla, y_gm, x, mod1, mod2, w_out.astype(BF16), ln_g.reshape(1, -1), ln_b.reshape(1, -1)]
    out_specs = [pl.BlockSpec((1, tm, D_MODEL), row), pl.BlockSpec((1, tm, D_MODEL), row)]
    out_shape = [jax.ShapeDtypeStruct((nb, t, D_MODEL), F32),
                 jax.ShapeDtypeStruct((nb, t, D_MODEL), F32 if route else BF16)]
    scratch = []
    if route:
        wr = jnp.pad(w_router, ((0, 0), (0, LANES - N_EXPERTS))).astype(BF16)
        in_specs.append(pl.BlockSpec((D_MODEL, LANES), const2))
        args.append(wr)
        out_specs += [pl.BlockSpec((1, tm, LANES), row), pl.BlockSpec((8, LANES), const2)]
        out_shape += [jax.ShapeDtypeStruct((nb, t, LANES), F32), jax.ShapeDtypeStruct((8, LANES), F32)]
        scratch = [pltpu.VMEM((1, LANES), F32)]
    return pl.pallas_call(
        functools.partial(_outproj_kernel, route=route, tm=tm),
        grid=(nb, t // tm),
        in_specs=in_specs, out_specs=out_specs, out_shape=out_shape, scratch_shapes=scratch,
        compiler_params=_cparams(("arbitrary", "arbitrary")),
        name="outproj_route" if route else "outproj",
    )(*args)


def _swiglu_acc(h, w1_ref, w3_ref, w2_ref, widx, side_work=None):
    acc = None
    for j in range(D_FF // FF_CHUNK):
        lo = j * FF_CHUNK
        if side_work is not None:
            side_work(j)
        a = _dot(h, w1_ref[widx + (slice(None), slice(lo, lo + FF_CHUNK))])
        g = _dot(h, w3_ref[widx + (slice(None), slice(lo, lo + FF_CHUNK))])
        p = (_silu(a) * g).astype(BF16)
        part = _dot(p, w2_ref[widx + (slice(lo, lo + FF_CHUNK), slice(None))])
        acc = part if acc is None else acc + part
    return acc


def _ffn_kernel(h_ref, x_ref, mod_ref, w1_ref, w3_ref, w2_ref, g_ref, b_ref, o_ref):
    y = _swiglu_acc(h_ref[0], w1_ref, w3_ref, w2_ref, ())
    gate = mod_ref[0][:, 2 * D_MODEL:3 * D_MODEL]
    o_ref[0] = _residual_ln(x_ref[0], y, gate, g_ref[...], b_ref[...])


def _ffn(h, x, mod, w1, w3, w2, ln_g, ln_b, *, tm):
    nb, t, _ = x.shape
    row = lambda b, i: (b, i, 0)
    const2 = lambda b, i: (0, 0)
    wspec = lambda shape: pl.BlockSpec(shape, const2, pipeline_mode=pl.Buffered(1))
    return pl.pallas_call(
        _ffn_kernel,
        grid=(nb, t // tm),
        in_specs=[
            pl.BlockSpec((1, tm, D_MODEL), row), pl.BlockSpec((1, tm, D_MODEL), row),
            pl.BlockSpec((1, 1, 3 * D_MODEL), lambda b, i: (b, 0, 0)),
            wspec((D_MODEL, D_FF)), wspec((D_MODEL, D_FF)), wspec((D_FF, D_MODEL)),
            pl.BlockSpec((1, D_MODEL), const2), pl.BlockSpec((1, D_MODEL), const2),
        ],
        out_specs=pl.BlockSpec((1, tm, D_MODEL), row),
        out_shape=jax.ShapeDtypeStruct((nb, t, D_MODEL), F32),
        compiler_params=_cparams(("arbitrary", "arbitrary")),
        name="ffn_dense",
    )(h, x, mod, w1.astype(BF16), w3.astype(BF16), w2.astype(BF16), ln_g.reshape(1, -1), ln_b.reshape(1, -1))


def _invert_kernel(slot_ref, init_ref, inv_ref, sem, *, n_pairs):
    cp = pltpu.make_async_copy(init_ref, inv_ref, sem)
    cp.start()
    cp.wait()

    def body(i, c):
        inv_ref[slot_ref[i]] = i
        return c

    lax.fori_loop(0, n_pairs, body, 0, unroll=8)


def _moe_invert(slots, inv_init):
    return pl.pallas_call(
        functools.partial(_invert_kernel, n_pairs=slots.shape[0]),
        grid_spec=pltpu.PrefetchScalarGridSpec(
            num_scalar_prefetch=1,
            grid=(1,),
            in_specs=[pl.BlockSpec(memory_space=pl.ANY)],
            out_specs=pl.BlockSpec(memory_space=pltpu.SMEM),
            scratch_shapes=[pltpu.SemaphoreType.DMA(())],
        ),
        out_shape=jax.ShapeDtypeStruct(inv_init.shape, jnp.int32),
        compiler_params=_cparams(("arbitrary",)),
        name="moe_invert",
    )(slots, inv_init)


def _moe_ffn_kernel(te_ref, inv_ref, h_hbm, w1_ref, w3_ref, w2_ref, y_hbm, xbuf, obuf, gsem, ssem, *, tg, n_tok):
    del te_ref
    j = pl.program_id(0)
    last = pl.num_programs(0) - 1
    slot = lax.rem(j, 2)
    other = 1 - slot

    def gather_row(tile, r, sl):
        tok = inv_ref[(tile + 1) * tg + r] & (n_tok - 1)
        return pltpu.make_async_copy(h_hbm.at[pl.ds(tok, 1), :], xbuf.at[sl, pl.ds(r, 1), :], gsem.at[sl])

    def scatter_row(tile, r, sl):
        dst = inv_ref[(tile + 1) * tg + r]
        return pltpu.make_async_copy(obuf.at[sl, pl.ds(r, 1), :], y_hbm.at[pl.ds(dst, 1), :], ssem.at[sl])

    def wait_gather(sl):
        pltpu.make_async_copy(h_hbm.at[pl.ds(0, tg), :], xbuf.at[sl], gsem.at[sl]).wait()

    def wait_scatter(sl):
        pltpu.make_async_copy(obuf.at[sl], y_hbm.at[pl.ds(0, tg), :], ssem.at[sl]).wait()

    @pl.when(j == 0)
    def _():
        obuf[1] = jnp.zeros((tg, D_MODEL), F32)

        def first(r, c):
            gather_row(0, r, 0).start()
            return c

        lax.fori_loop(0, tg, first, 0)

    @pl.when(j > 0)
    def _():
        wait_scatter(slot)

    wait_gather(slot)
    x = xbuf[slot].astype(BF16)
    nxt = jnp.minimum(j + 1, last)
    n_chunks = D_FF // FF_CHUNK
    per = -(-tg // n_chunks)

    def side_work(c):
        for r in range(c * per, min((c + 1) * per, tg)):
            gather_row(nxt, r, other).start()
            scatter_row(j - 1, r, other).start(priority=1)

    obuf[slot] = _swiglu_acc(x, w1_ref, w3_ref, w2_ref, (0,), side_work)

    @pl.when(j == last)
    def _():
        wait_gather(other)
        wait_scatter(other)

        def final(r, c):
            scatter_row(j, r, slot).start()
            return c

        lax.fori_loop(0, tg, final, 0)
        wait_scatter(slot)


def _moe_ffn(tile_expert, inv, h, w1, w3, w2, *, tg, n_out):
    n_tiles = tile_expert.shape[0]
    wmap = lambda j, te, iv: (te[j], 0, 0)
    return pl.pallas_call(
        functools.partial(_moe_ffn_kernel, tg=tg, n_tok=h.shape[0]),
        grid_spec=pltpu.PrefetchScalarGridSpec(
            num_scalar_prefetch=2,
            grid=(n_tiles,),
            in_specs=[
                pl.BlockSpec(memory_space=pl.ANY),
                pl.BlockSpec((1, D_MODEL, D_FF), wmap),
                pl.BlockSpec((1, D_MODEL, D_FF), wmap),
                pl.BlockSpec((1, D_FF, D_MODEL), wmap),
            ],
            out_specs=pl.BlockSpec(memory_space=pl.ANY),
            scratch_shapes=[
                pltpu.VMEM((2, tg, D_MODEL), F32), pltpu.VMEM((2, tg, D_MODEL), F32),
                pltpu.SemaphoreType.DMA((2,)), pltpu.SemaphoreType.DMA((2,)),
            ],
        ),
        out_shape=jax.ShapeDtypeStruct((n_out, D_MODEL), F32),
        compiler_params=_cparams(("arbitrary",)),
        name="moe_ffn",
    )(tile_expert, inv, h, w1.astype(BF16), w3.astype(BF16), w2.astype(BF16))


def _combine_kernel(y1_ref, y2_ref, rt_ref, x_ref, mod_ref, g_ref, b_ref, o_ref):
    rt = rt_ref[0]
    y = rt[:, 4:5] * y1_ref[...] + rt[:, 5:6] * y2_ref[...]
    gate = mod_ref[0][:, 2 * D_MODEL:3 * D_MODEL]
    o_ref[0] = _residual_ln(x_ref[0], y, gate, g_ref[...], b_ref[...])


def _combine(y2, route, x, mod, ln_g, ln_b, *, tm):
    nb, t, _ = x.shape
    row = lambda b, i: (b, i, 0)
    const2 = lambda b, i: (0, 0)
    nt = t // tm
    return pl.pallas_call(
        _combine_kernel,
        grid=(nb, nt),
        in_specs=[
            pl.BlockSpec((tm, D_MODEL), lambda b, i: (b * nt + i, 0)),
            pl.BlockSpec((tm, D_MODEL), lambda b, i: (nb * nt + b * nt + i, 0)),
            pl.BlockSpec((1, tm, LANES), row),
            pl.BlockSpec((1, tm, D_MODEL), row),
            pl.BlockSpec((1, 1, 3 * D_MODEL), lambda b, i: (b, 0, 0)),
            pl.BlockSpec((1, D_MODEL), const2), pl.BlockSpec((1, D_MODEL), const2),
        ],
        out_specs=pl.BlockSpec((1, tm, D_MODEL), row),
        out_shape=jax.ShapeDtypeStruct((nb, t, D_MODEL), F32),
        compiler_params=_cparams(("arbitrary", "arbitrary")),
        name="moe_combine",
    )(y2, y2, route, x, mod, ln_g.reshape(1, -1), ln_b.reshape(1, -1))


def _moe_plan(route, counts, *, tg, n_tiles):
    n_pairs = 2 * route.shape[0] * route.shape[1]
    cnt = counts[0, :N_EXPERTS].astype(jnp.int32)
    tiles = (cnt + tg - 1) // tg
    tile_end = jnp.cumsum(tiles)
    start = (tile_end - tiles) * tg
    r = route.reshape(-1, LANES)
    e = r[:, 0:2].astype(jnp.int32)
    assert (n_pairs // 2) & (n_pairs // 2 - 1) == 0, "token count must be a power of two (pair ids are masked)"
    slots = jnp.transpose(start[e] + r[:, 2:4].astype(jnp.int32)).reshape(-1) + tg
    n_active = tile_end[-1]
    j = jnp.minimum(jnp.arange(n_tiles, dtype=jnp.int32), n_active - 1)
    tile_expert = jnp.sum((j[:, None] >= tile_end[None, :]).astype(jnp.int32), axis=1)
    s = jnp.arange(n_tiles * tg, dtype=jnp.int32)
    es = jnp.minimum(jnp.sum((s[:, None] >= (tile_end * tg)[None, :]).astype(jnp.int32), axis=1), N_EXPERTS - 1)
    pad_before = jnp.cumsum(tiles * tg - cnt) - (tiles * tg - cnt)
    in_group = s - start[es]
    pad_id = jnp.where(s >= n_active * tg, s - n_pairs, pad_before[es] + in_group - cnt[es])
    n_spare = n_tiles * tg - n_pairs
    body = n_pairs + jnp.clip(pad_id, 0, n_spare - 1)
    head = n_pairs + n_spare + jnp.arange(tg, dtype=jnp.int32)
    return slots, tile_expert, jnp.concatenate([head, body]), n_pairs + n_spare + tg


def _w_in_padded(w_in):
    z, xs, bm, cm = (0, 512), (512, 1024), (1024, 1152), (1152, 1280)
    dt, ql, kvl, kpe, gu, gv = (1280, 1288), (1288, 1544), (1544, 1672), (1672, 1704), (1704, 1960), (1960, 2216)
    seg = lambda ab: w_in[:, ab[0]:ab[1]]
    k_pe = seg(kpe)
    k_sw = jnp.concatenate([k_pe[:, MLA_ROPE // 2:], k_pe[:, :MLA_ROPE // 2]], axis=1)
    pad = jnp.zeros((D_MODEL, LANES - 2 * MLA_ROPE - SSD_HEADS), w_in.dtype)
    cols = [seg(z), seg(xs), seg(bm), seg(cm), seg(ql), seg(kvl), seg(gu), seg(gv), k_pe, k_sw, seg(dt), pad]
    return jnp.concatenate(cols, axis=1).astype(BF16)


def _forward(x, c, ln0_g, ln0_b, ada_w, ada_b, post_ln_g, post_ln_b, w_in, ssd_conv_w, ssd_conv_b,
             ssd_dt_bias, ssd_a_log, ssd_d, ssd_norm_w, mla_q_norm, mla_w_qb, mla_kv_norm, mla_w_kvb,
             gm_ln_g, gm_ln_b, gm_w_s, gm_b_s, w_out, ffn_w1, ffn_w3, ffn_w2,
             moe_router, moe_w1, moe_w3, moe_w2, *, tm, tq, tg, tc):
    nb, t, _ = x.shape
    mods = _modulations(c, ada_w, ada_b).reshape(2 * DEPTH, nb, 1, 3 * D_MODEL)
    tables = _rope_tables(t)
    for layer in range(DEPTH):
        mod_mix, mod_ffn = mods[2 * layer], mods[2 * layer + 1]
        w_pad = _w_in_padded(w_in[layer])
        if layer == 0:
            x, p_ssd, p_mla, p_gm, p_misc = _inproj(x, mod_mix, w_pad, ln0_g, ln0_b, tm=tm)
        else:
            p_ssd, p_mla, p_gm, p_misc = _inproj(x, mod_mix, w_pad, tm=tm)
        y_ssd = _ssd(p_ssd, p_misc, ssd_conv_w[layer], ssd_conv_b[layer], ssd_dt_bias[layer],
                     ssd_a_log[layer], ssd_d[layer], ssd_norm_w[layer], ts=tm)
        q, k, v = _mla_prep(p_mla, p_misc, mla_q_norm[layer], mla_kv_norm[layer], mla_w_qb[layer],
                            mla_w_kvb[layer], tables, tt=tm)
        y_mla = _flash(q, k, v, tq=tq)
        y_gm = _gmlp(p_gm, gm_ln_g[layer], gm_ln_b[layer], gm_w_s[layer], gm_b_s[layer], ts=tm)
        i = layer // 2
        if layer % 2 == 0:
            x, h = _outproj(y_ssd, y_mla, y_gm, x, mod_mix, mod_ffn, w_out[layer],
                            post_ln_g[layer, 0], post_ln_b[layer, 0], tm=tm)
            x = _ffn(h, x, mod_ffn, ffn_w1[i], ffn_w3[i], ffn_w2[i], post_ln_g[layer, 1], post_ln_b[layer, 1], tm=tm)
        else:
            x, h, route, counts = _outproj(y_ssd, y_mla, y_gm, x, mod_mix, mod_ffn, w_out[layer],
                                           post_ln_g[layer, 0], post_ln_b[layer, 0], moe_router[i], tm=tm)
            n_tiles = (2 * nb * t) // tg + N_EXPERTS
            slots, tile_expert, inv_init, n_out = _moe_plan(route, counts, tg=tg, n_tiles=n_tiles)
            inv = _moe_invert(slots, inv_init)
            y2 = _moe_ffn(tile_expert, inv, h.reshape(nb * t, D_MODEL), moe_w1[i], moe_w3[i], moe_w2[i],
                          tg=tg, n_out=n_out)
            x = _combine(y2, route, x, mod_ffn, post_ln_g[layer, 1], post_ln_b[layer, 1], tm=tc)
    return x


def kernel(x, c, ln0_g, ln0_b, ada_w, ada_b, post_ln_g, post_ln_b, w_in, ssd_conv_w, ssd_conv_b, ssd_dt_bias, ssd_a_log, ssd_d, ssd_norm_w, mla_q_norm, mla_w_qb, mla_kv_norm, mla_w_kvb, gm_ln_g, gm_ln_b, gm_w_s, gm_b_s, w_out, ffn_w1, ffn_w3, ffn_w2, moe_router, moe_w1, moe_w3, moe_w2):
    t = x.shape[1]
    tm = min(512, t)
    return _forward(x, c, ln0_g, ln0_b, ada_w, ada_b, post_ln_g, post_ln_b, w_in, ssd_conv_w, ssd_conv_b,
                    ssd_dt_bias, ssd_a_log, ssd_d, ssd_norm_w, mla_q_norm, mla_w_qb, mla_kv_norm, mla_w_kvb,
                    gm_ln_g, gm_ln_b, gm_w_s, gm_b_s, w_out, ffn_w1, ffn_w3, ffn_w2,
                    moe_router, moe_w1, moe_w3, moe_w2, tm=tm, tq=min(1024, t), tg=tm, tc=tm)
```

```python
import functools
import math

import numpy as np
import jax
import jax.numpy as jnp
from jax import lax
from jax.experimental import pallas as pl
from jax.experimental.pallas import tpu as pltpu

F32 = jnp.float32
BF16 = jnp.bfloat16

D_MODEL = 1024
DEPTH = 2
SSD_INNER = 512
SSD_HEAD_DIM = 64
SSD_HEADS = 8
SSD_GROUPS = 2
SSD_STATE = 64
SSD_CONV = 4
CHUNK = 128
SSD_CONV_DIM = 768
MLA_HEADS = 4
MLA_Q_RANK = 256
MLA_KV_RANK = 128
MLA_NOPE = 64
MLA_ROPE = 32
MLA_V = 64
MLA_SCALE = (MLA_NOPE + MLA_ROPE) ** -0.5
ROPE_THETA = 10000.0
GM_GROUPS = 4
GM_GROUP_DIM = 64
GM_WIDTH = 256
D_FF = 2816
N_EXPERTS = 8
DN_ALPHA = (2 * DEPTH) ** 0.25
LN_EPS = 1e-5
RMS_EPS = 1e-6

LANES = 128
FF_CHUNK = 256
VMEM_LIMIT = 56 * 1024 * 1024

MISC_KPE = 0
MISC_KPE_SW = 32
MISC_DT = 64
D_IN_PAD = 1280 + 384 + 512 + 128


def _cparams(sem):
    return pltpu.CompilerParams(dimension_semantics=sem, vmem_limit_bytes=VMEM_LIMIT)


def _silu(x):
    return x * jax.nn.sigmoid(x)


def _gelu(x):
    return 0.5 * x * (1.0 + lax.erf(x * (2.0 ** -0.5)))


def _softplus(x):
    return jnp.maximum(x, 0.0) + jnp.log1p(jnp.exp(-jnp.abs(x)))


def _layer_norm(x, g, b):
    mu = jnp.mean(x, axis=-1, keepdims=True)
    xc = x - mu
    var = jnp.mean(xc * xc, axis=-1, keepdims=True)
    return xc * lax.rsqrt(var + LN_EPS) * g + b


def _rms(x):
    return x * lax.rsqrt(jnp.mean(x * x, axis=-1, keepdims=True) + RMS_EPS)


def _dot(a, b):
    return jnp.dot(a, b, preferred_element_type=F32)


def _dot_nt(a, b):
    return lax.dot_general(a, b, (((1,), (1,)), ((), ())), preferred_element_type=F32)


def _dot_tn(a, b):
    return lax.dot_general(a, b, (((0,), (0,)), ((), ())), preferred_element_type=F32)


def _mod_kernel(c_ref, w_ref, b_ref, o_ref):
    s = _silu(c_ref[...]).astype(BF16)
    o_ref[0] = _dot(s, w_ref[0].astype(BF16)) + b_ref[0]


def _modulations(c, ada_w, ada_b):
    nb = c.shape[0]
    n = ada_w.shape[0] * ada_w.shape[1]
    w = ada_w.reshape(n, D_MODEL, 3 * D_MODEL)
    b = ada_b.reshape(n, 1, 3 * D_MODEL)
    tn = 1024
    return pl.pallas_call(
        _mod_kernel,
        grid=(n, 3 * D_MODEL // tn),
        in_specs=[
            pl.BlockSpec((nb, D_MODEL), lambda j, k: (0, 0)),
            pl.BlockSpec((1, D_MODEL, tn), lambda j, k: (j, 0, k)),
            pl.BlockSpec((1, 1, tn), lambda j, k: (j, 0, k)),
        ],
        out_specs=pl.BlockSpec((1, nb, tn), lambda j, k: (j, 0, k)),
        out_shape=jax.ShapeDtypeStruct((n, nb, 3 * D_MODEL), F32),
        compiler_params=_cparams(("arbitrary", "arbitrary")),
        name="adaln_mod",
    )(c, w, b)


def _inproj_kernel(*refs, do_ln):
    if do_ln:
        x_ref, mod_ref, g_ref, b_ref, w_ref, xo_ref, ssd_ref, mla_ref, gm_ref, misc_ref = refs
    else:
        x_ref, mod_ref, w_ref, ssd_ref, mla_ref, gm_ref, misc_ref = refs
    x = x_ref[0]
    if do_ln:
        x = _layer_norm(x, g_ref[...], b_ref[...])
        xo_ref[0] = x
    m = mod_ref[0]
    h = x * (1.0 + m[:, D_MODEL:2 * D_MODEL]) + m[:, 0:D_MODEL]
    p = _dot(h.astype(BF16), w_ref[...])
    ssd_ref[0] = p[:, 0:1280].astype(BF16)
    mla_ref[0] = p[:, 1280:1664].astype(BF16)
    gm_ref[0] = p[:, 1664:2176].astype(BF16)
    misc_ref[0] = p[:, 2176:2304]


def _inproj(x, mod, w_pad, ln_g=None, ln_b=None, *, tm):
    nb, t, _ = x.shape
    do_ln = ln_g is not None
    row = lambda b, i: (b, i, 0)
    const2 = lambda b, i: (0, 0)
    in_specs = [pl.BlockSpec((1, tm, D_MODEL), row), pl.BlockSpec((1, 1, 3 * D_MODEL), lambda b, i: (b, 0, 0))]
    args = [x, mod]
    if do_ln:
        in_specs += [pl.BlockSpec((1, D_MODEL), const2), pl.BlockSpec((1, D_MODEL), const2)]
        args += [ln_g.reshape(1, D_MODEL), ln_b.reshape(1, D_MODEL)]
    in_specs.append(pl.BlockSpec((D_MODEL, D_IN_PAD), const2))
    args.append(w_pad)
    out_specs = [pl.BlockSpec((1, tm, 1280), row), pl.BlockSpec((1, tm, 384), row),
                 pl.BlockSpec((1, tm, 512), row), pl.BlockSpec((1, tm, LANES), row)]
    out_shape = [jax.ShapeDtypeStruct((nb, t, 1280), BF16), jax.ShapeDtypeStruct((nb, t, 384), BF16),
                 jax.ShapeDtypeStruct((nb, t, 512), BF16), jax.ShapeDtypeStruct((nb, t, LANES), F32)]
    if do_ln:
        out_specs = [pl.BlockSpec((1, tm, D_MODEL), row)] + out_specs
        out_shape = [jax.ShapeDtypeStruct((nb, t, D_MODEL), F32)] + out_shape
    return pl.pallas_call(
        functools.partial(_inproj_kernel, do_ln=do_ln),
        grid=(nb, t // tm),
        in_specs=in_specs, out_specs=out_specs, out_shape=out_shape,
        compiler_params=_cparams(("arbitrary", "arbitrary")),
        name="inproj_ln" if do_ln else "inproj",
    )(*args)


def _ssd_kernel(ssd_ref, misc_ref, cw_ref, cb_ref, dtb_ref, arow_ref, drow_ref, nw_ref, o_ref,
                xpad_ref, st_ref, y_ref, *, ts):
    L = CHUNK

    @pl.when(pl.program_id(1) == 0)
    def _():
        xpad_ref[0:8, :] = jnp.zeros((8, SSD_CONV_DIM), F32)
        st_ref[...] = jnp.zeros(st_ref.shape, F32)

    blk = ssd_ref[0]
    xpad_ref[8:8 + ts, :] = blk[:, SSD_INNER:SSD_INNER + SSD_CONV_DIM].astype(F32)
    cw = cw_ref[...]
    acc = cb_ref[...] + cw[3:4, :] * xpad_ref[8:8 + ts, :]
    for k in range(SSD_CONV - 1):
        acc = acc + cw[k:k + 1, :] * xpad_ref[pl.ds(5 + k, ts), :]
    xpad_ref[0:8, :] = xpad_ref[ts:ts + 8, :]
    xbc = _silu(acc)

    dtv_all = _softplus(misc_ref[0] + dtb_ref[...])
    a_all = dtv_all * arow_ref[...]
    row = lax.broadcasted_iota(jnp.int32, (L, LANES), 0)
    col = lax.broadcasted_iota(jnp.int32, (L, LANES), 1)
    causal = row >= col
    lo_half = col < SSD_HEAD_DIM
    lo_half_row = lo_half[0:1, :]
    heads_per_group = SSD_HEADS // SSD_GROUPS

    for c in range(ts // L):
        r0 = c * L
        xs = xbc[r0:r0 + L, 0:SSD_INNER]
        bm = xbc[r0:r0 + L, SSD_INNER:SSD_INNER + 128]
        cm = xbc[r0:r0 + L, SSD_INNER + 128:SSD_INNER + 256]
        dtv = dtv_all[r0:r0 + L, :]
        acs = a_all[r0:r0 + L, :]
        s = 1
        while s < L:
            acs = acs + jnp.where(row >= s, pltpu.roll(acs, s, 0), 0.0)
            s *= 2
        acs_t = acs.T
        atot = acs[L - 1:L, :]
        e_out = jnp.exp(acs)
        e_end = jnp.exp(atot - acs)
        e_tot = jnp.exp(atot)

        for g in range(SSD_GROUPS):
            in_group = (col >= SSD_STATE * g) & (col < SSD_STATE * (g + 1))
            bg = jnp.where(in_group, bm, 0.0).astype(BF16)
            cg = jnp.where(in_group, cm, 0.0).astype(BF16)
            cb = _dot_nt(cg, bg)
            for q in range(heads_per_group // 2):
                pair = g * (heads_per_group // 2) + q
                l0 = MISC_DT + 2 * pair
                l1 = l0 + 1
                lanes = slice(LANES * pair, LANES * pair + LANES)

                def pair_cols(arr):
                    return jnp.where(lo_half, arr[:, l0:l0 + 1], arr[:, l1:l1 + 1])

                x_p = xs[:, lanes]
                xdt = x_p * pair_cols(dtv)
                xdt_b = xdt.astype(BF16)
                halves = []
                for ll in (l0, l1):
                    decay = jnp.exp(jnp.where(causal, acs[:, ll:ll + 1] - acs_t[ll:ll + 1, :], -jnp.inf))
                    halves.append(_dot((cb * decay).astype(BF16), xdt_b))
                y = jnp.where(lo_half, halves[0], halves[1])
                st = st_ref[pair]
                y = y + pair_cols(e_out) * _dot(cg, st.astype(BF16)) + x_p * drow_ref[:, lanes]
                wst = (xdt * pair_cols(e_end)).astype(BF16)
                tot = jnp.where(lo_half_row, e_tot[:, l0:l0 + 1], e_tot[:, l1:l1 + 1])
                st_ref[pair] = tot * st + _dot_tn(bg, wst)
                y_ref[r0:r0 + L, lanes] = y

    gy = y_ref[...] * _silu(blk[:, 0:SSD_INNER].astype(F32))
    half = SSD_INNER // SSD_GROUPS
    for g in range(SSD_GROUPS):
        seg = gy[:, half * g:half * g + half]
        o_ref[0, :, half * g:half * g + half] = (_rms(seg) * nw_ref[:, half * g:half * g + half]).astype(BF16)


def _ssd(ssd_in, misc, conv_w, conv_b, dt_bias, a_log, d_skip, norm_w, *, ts):
    nb, t, _ = ssd_in.shape
    lane_pad = (MISC_DT, LANES - MISC_DT - SSD_HEADS)
    dtb = jnp.pad(dt_bias.astype(F32), lane_pad).reshape(1, LANES)
    arow = jnp.pad(-jnp.exp(a_log.astype(F32)), lane_pad).reshape(1, LANES)
    drow = jnp.repeat(d_skip.astype(F32), SSD_HEAD_DIM).reshape(1, SSD_INNER)
    row = lambda b, i: (b, i, 0)
    const2 = lambda b, i: (0, 0)
    return pl.pallas_call(
        functools.partial(_ssd_kernel, ts=ts),
        grid=(nb, t // ts),
        in_specs=[
            pl.BlockSpec((1, ts, 1280), row),
            pl.BlockSpec((1, ts, LANES), row),
            pl.BlockSpec((SSD_CONV, SSD_CONV_DIM), const2),
            pl.BlockSpec((1, SSD_CONV_DIM), const2),
            pl.BlockSpec((1, LANES), const2),
            pl.BlockSpec((1, LANES), const2),
            pl.BlockSpec((1, SSD_INNER), const2),
            pl.BlockSpec((1, SSD_INNER), const2),
        ],
        out_specs=pl.BlockSpec((1, ts, SSD_INNER), row),
        out_shape=jax.ShapeDtypeStruct((nb, t, SSD_INNER), BF16),
        scratch_shapes=[
            pltpu.VMEM((ts + 8, SSD_CONV_DIM), F32),
            pltpu.VMEM((SSD_HEADS // 2, SSD_GROUPS * SSD_STATE, 2 * SSD_HEAD_DIM), F32),
            pltpu.VMEM((ts, SSD_INNER), F32),
        ],
        compiler_params=_cparams(("arbitrary", "arbitrary")),
        name="ssd_scan",
    )(ssd_in, misc, conv_w, conv_b.reshape(1, -1), dtb, arow, drow, norm_w.reshape(1, -1))


def _gmlp_kernel(gm_ref, lg_ref, lb_ref, ws_ref, bs_ref, o_ref, *, ts):
    blk = gm_ref[0].astype(F32)
    gu = _gelu(blk[:, 0:GM_WIDTH])
    gv = _layer_norm(_gelu(blk[:, GM_WIDTH:2 * GM_WIDTH]), lg_ref[...], lb_ref[...]).astype(BF16)
    row = lax.broadcasted_iota(jnp.int32, (CHUNK, CHUNK), 0)
    col = lax.broadcasted_iota(jnp.int32, (CHUNK, CHUNK), 1)
    ws = [jnp.where(row >= col, ws_ref[g], 0.0).astype(BF16) for g in range(GM_GROUPS)]
    group = lax.broadcasted_iota(jnp.int32, (CHUNK, GM_WIDTH), 1) // GM_GROUP_DIM
    for c in range(ts // CHUNK):
        r0 = c * CHUNK
        v = gv[r0:r0 + CHUNK, :]
        s = _dot(ws[GM_GROUPS - 1], v)
        for g in range(GM_GROUPS - 2, -1, -1):
            s = jnp.where(group == g, _dot(ws[g], v), s)
        o_ref[0, r0:r0 + CHUNK, :] = (gu[r0:r0 + CHUNK, :] * (s + bs_ref[...])).astype(BF16)


def _gmlp(gm_in, ln_g, ln_b, w_s, b_s, *, ts):
    nb, t, _ = gm_in.shape
    bs = jnp.repeat(jnp.transpose(b_s), GM_GROUP_DIM, axis=1)
    row = lambda b, i: (b, i, 0)
    return pl.pallas_call(
        functools.partial(_gmlp_kernel, ts=ts),
        grid=(nb, t // ts),
        in_specs=[
            pl.BlockSpec((1, ts, 2 * GM_WIDTH), row),
            pl.BlockSpec((1, GM_WIDTH), lambda b, i: (0, 0)),
            pl.BlockSpec((1, GM_WIDTH), lambda b, i: (0, 0)),
            pl.BlockSpec((GM_GROUPS, CHUNK, CHUNK), lambda b, i: (0, 0, 0)),
            pl.BlockSpec((CHUNK, GM_WIDTH), lambda b, i: (0, 0)),
        ],
        out_specs=pl.BlockSpec((1, ts, GM_WIDTH), row),
        out_shape=jax.ShapeDtypeStruct((nb, t, GM_WIDTH), BF16),
        compiler_params=_cparams(("arbitrary", "arbitrary")),
        name="gmlp_gate",
    )(gm_in, ln_g.reshape(1, -1), ln_b.reshape(1, -1), w_s, bs)


def _mla_prep_kernel(mla_ref, misc_ref, qn_ref, kvn_ref, wqa_ref, wqb_ref, wk_ref, wv_ref,
                     cq_ref, sq_ref, ck_ref, sk_ref, q_ref, k_ref, v_ref):
    m = mla_ref[0].astype(F32)
    qn = (_rms(m[:, 0:MLA_Q_RANK]) * qn_ref[...]).astype(BF16)
    kvn = (_rms(m[:, MLA_Q_RANK:MLA_Q_RANK + MLA_KV_RANK]) * kvn_ref[...]).astype(BF16)
    qa = _dot(qn, wqa_ref[...])
    qb = _dot(qn, wqb_ref[...])
    kk = _dot(kvn, wk_ref[...])
    vv = _dot(kvn, wv_ref[...])
    misc = misc_ref[0]
    kr = pltpu.roll(misc, 64 - MISC_KPE, 1) * ck_ref[...] + pltpu.roll(misc, 64 - MISC_KPE_SW, 1) * sk_ref[...]
    cq = cq_ref[...]
    sq = sq_ref[...]
    ones_lane = lax.broadcasted_iota(jnp.int32, (vv.shape[0], LANES), 1) == MLA_V
    for h in range(MLA_HEADS):
        lo = LANES * h
        q_ref[0, h] = (qa[:, lo:lo + LANES] * cq + qb[:, lo:lo + LANES] * sq).astype(BF16)
        k_ref[0, h] = (kk[:, lo:lo + LANES] + kr).astype(BF16)
        v_ref[0, h] = jnp.where(ones_lane, 1.0, vv[:, lo:lo + LANES]).astype(BF16)


def _rope_tables(t):
    inv = ROPE_THETA ** (-jnp.arange(0, MLA_ROPE, 2, dtype=F32) / MLA_ROPE)
    ang = jnp.arange(t, dtype=F32)[:, None] * inv[None, :]
    cos, sin = jnp.cos(ang), jnp.sin(ang)
    z64 = jnp.zeros((t, 64), F32)
    z32 = jnp.zeros((t, 32), F32)
    one64 = jnp.ones((t, 64), F32)
    qscale = MLA_SCALE * math.log2(math.e)
    cq = jnp.concatenate([one64, cos, cos, z32], axis=1) * qscale
    sq = jnp.concatenate([z64, -sin, sin, z32], axis=1) * qscale
    ck = jnp.concatenate([z64, cos, cos, z32], axis=1)
    sk = jnp.concatenate([z64, -sin, sin, z32], axis=1)
    return cq, sq, ck, sk


def _mla_weights(w_qb, w_kvb):
    dq = MLA_NOPE + MLA_ROPE
    wq = w_qb.reshape(MLA_Q_RANK, MLA_HEADS, dq)
    nope, pe = wq[..., :MLA_NOPE], wq[..., MLA_NOPE:]
    pe_sw = jnp.concatenate([pe[..., MLA_ROPE // 2:], pe[..., :MLA_ROPE // 2]], axis=-1)
    zpad = jnp.zeros((MLA_Q_RANK, MLA_HEADS, LANES - dq), w_qb.dtype)
    wqa = jnp.concatenate([nope, pe, zpad], axis=-1).reshape(MLA_Q_RANK, MLA_HEADS * LANES)
    wqb = jnp.concatenate([jnp.zeros_like(nope), pe_sw, zpad], axis=-1).reshape(MLA_Q_RANK, MLA_HEADS * LANES)
    wkv = w_kvb.reshape(MLA_KV_RANK, MLA_HEADS, MLA_NOPE + MLA_V)
    wk = jnp.concatenate([wkv[..., :MLA_NOPE], jnp.zeros((MLA_KV_RANK, MLA_HEADS, LANES - MLA_NOPE), w_kvb.dtype)],
                         axis=-1).reshape(MLA_KV_RANK, MLA_HEADS * LANES)
    wv = jnp.concatenate([wkv[..., MLA_NOPE:], jnp.zeros((MLA_KV_RANK, MLA_HEADS, LANES - MLA_V), w_kvb.dtype)],
                         axis=-1).reshape(MLA_KV_RANK, MLA_HEADS * LANES)
    return wqa.astype(BF16), wqb.astype(BF16), wk.astype(BF16), wv.astype(BF16)


def _mla_prep(mla_in, misc, q_norm, kv_norm, w_qb, w_kvb, tables, *, tt):
    nb, t, _ = mla_in.shape
    wqa, wqb, wk, wv = _mla_weights(w_qb, w_kvb)
    row = lambda b, i: (b, i, 0)
    const2 = lambda b, i: (0, 0)
    tab = pl.BlockSpec((tt, LANES), lambda b, i: (i, 0))
    hrow = lambda b, i: (b, 0, i, 0)
    return pl.pallas_call(
        _mla_prep_kernel,
        grid=(nb, t // tt),
        in_specs=[
            pl.BlockSpec((1, tt, 384), row),
            pl.BlockSpec((1, tt, LANES), row),
            pl.BlockSpec((1, MLA_Q_RANK), const2),
            pl.BlockSpec((1, MLA_KV_RANK), const2),
            pl.BlockSpec((MLA_Q_RANK, MLA_HEADS * LANES), const2),
            pl.BlockSpec((MLA_Q_RANK, MLA_HEADS * LANES), const2),
            pl.BlockSpec((MLA_KV_RANK, MLA_HEADS * LANES), const2),
            pl.BlockSpec((MLA_KV_RANK, MLA_HEADS * LANES), const2),
            tab, tab, tab, tab,
        ],
        out_specs=[
            pl.BlockSpec((1, MLA_HEADS, tt, LANES), hrow),
            pl.BlockSpec((1, MLA_HEADS, tt, LANES), hrow),
            pl.BlockSpec((1, MLA_HEADS, tt, LANES), hrow),
        ],
        out_shape=[
            jax.ShapeDtypeStruct((nb, MLA_HEADS, t, LANES), BF16),
            jax.ShapeDtypeStruct((nb, MLA_HEADS, t, LANES), BF16),
            jax.ShapeDtypeStruct((nb, MLA_HEADS, t, LANES), BF16),
        ],
        compiler_params=_cparams(("arbitrary", "arbitrary")),
        name="mla_prep",
    )(mla_in, misc, q_norm.reshape(1, -1), kv_norm.reshape(1, -1), wqa, wqb, wk, wv, *tables)


def _flash_kernel(q_ref, k_ref, v_ref, o_ref, *, tq):
    i = pl.program_id(1)
    tri = lax.broadcasted_iota(jnp.int32, (tq, tq), 1) <= lax.broadcasted_iota(jnp.int32, (tq, tq), 0)
    def attend(q, kb, vb, m, acc, mask):
        s = _dot_nt(q, kb)
        if mask is not None:
            s = jnp.where(mask, s, -jnp.inf)
        m_new = jnp.maximum(m, jnp.max(s, axis=1, keepdims=True))
        p = jnp.exp2(s - m_new)
        return m_new, jnp.exp2(m - m_new) * acc + _dot(p.astype(BF16), vb)

    def step(j, carry, mask):
        start = pl.multiple_of(j * tq, tq)
        return tuple(attend(q_ref[0, h], k_ref[0, h, pl.ds(start, tq), :], v_ref[0, h, pl.ds(start, tq), :],
                            carry[h][0], carry[h][1], mask) for h in range(MLA_HEADS))

    init = tuple((jnp.full((tq, 1), -jnp.inf, F32), jnp.zeros((tq, LANES), F32)) for _ in range(MLA_HEADS))
    carry = lax.fori_loop(0, i, functools.partial(step, mask=None), init)
    carry = step(i, carry, tri)
    outs = [acc[:, 0:MLA_V] / acc[:, MLA_V:MLA_V + 1] for _, acc in carry]
    o_ref[0] = jnp.concatenate(outs, axis=-1).astype(BF16)


def _flash(q, k, v, *, tq):
    nb, nh, t, _ = q.shape
    return pl.pallas_call(
        functools.partial(_flash_kernel, tq=tq),
        grid=(nb, t // tq),
        in_specs=[
            pl.BlockSpec((1, nh, tq, LANES), lambda b, i: (b, 0, i, 0)),
            pl.BlockSpec((1, nh, t, LANES), lambda b, i: (b, 0, 0, 0)),
            pl.BlockSpec((1, nh, t, LANES), lambda b, i: (b, 0, 0, 0)),
        ],
        out_specs=pl.BlockSpec((1, tq, nh * MLA_V), lambda b, i: (b, i, 0)),
        out_shape=jax.ShapeDtypeStruct((nb, t, nh * MLA_V), BF16),
        compiler_params=_cparams(("arbitrary", "arbitrary")),
        name="mla_flash",
    )(q, k, v)


def _residual_ln(x, y, gate, g, b):
    return _layer_norm(DN_ALPHA * x + (1.0 + gate) * y, g, b)


def _outproj_kernel(*refs, route, tm):
    if route:
        (ys_ref, ym_ref, yg_ref, x_ref, mod1_ref, mod2_ref, w_ref, g_ref, b_ref, wr_ref,
         xo_ref, h_ref, rt_ref, cnt_ref, run_ref) = refs
    else:
        ys_ref, ym_ref, yg_ref, x_ref, mod1_ref, mod2_ref, w_ref, g_ref, b_ref, xo_ref, h_ref = refs
    y = (_dot(ys_ref[0], w_ref[0:512, :]) + _dot(ym_ref[0], w_ref[512:768, :])
         + _dot(yg_ref[0], w_ref[768:1024, :]))
    gate = mod1_ref[0][:, 2 * D_MODEL:3 * D_MODEL]
    xn = _residual_ln(x_ref[0], y, gate, g_ref[...], b_ref[...])
    xo_ref[0] = xn
    m2 = mod2_ref[0]
    h = xn * (1.0 + m2[:, D_MODEL:2 * D_MODEL]) + m2[:, 0:D_MODEL]
    if not route:
        h_ref[0] = h.astype(BF16)
        return
    h_ref[0] = h

    @pl.when((pl.program_id(0) == 0) & (pl.program_id(1) == 0))
    def _():
        run_ref[...] = jnp.zeros(run_ref.shape, F32)

    lane = lax.broadcasted_iota(jnp.int32, (tm, LANES), 1).astype(F32)
    logits = jnp.where(lane < N_EXPERTS, _dot(h.astype(BF16), wr_ref[...]), -jnp.inf)
    v1 = jnp.max(logits, axis=1, keepdims=True)
    i1 = jnp.min(jnp.where(logits == v1, lane, float(LANES)), axis=1, keepdims=True)
    rest = jnp.where(lane == i1, -jnp.inf, logits)
    v2 = jnp.max(rest, axis=1, keepdims=True)
    i2 = jnp.min(jnp.where(rest == v2, lane, float(LANES)), axis=1, keepdims=True)
    e2 = jnp.exp(v2 - v1)
    g1 = 1.0 / (1.0 + e2)
    g2 = e2 / (1.0 + e2)
    hot = ((lane == i1) | (lane == i2)).astype(F32)
    r_i = lax.broadcasted_iota(jnp.int32, (tm, tm), 0)
    c_i = lax.broadcasted_iota(jnp.int32, (tm, tm), 1)
    below = (c_i < r_i).astype(BF16)
    rank = _dot(below, hot.astype(BF16)) + run_ref[...]
    r1 = jnp.sum(jnp.where(lane == i1, rank, 0.0), axis=1, keepdims=True)
    r2 = jnp.sum(jnp.where(lane == i2, rank, 0.0), axis=1, keepdims=True)
    run = run_ref[...] + jnp.sum(hot, axis=0, keepdims=True)
    run_ref[...] = run
    cnt_ref[...] = jnp.broadcast_to(run, cnt_ref.shape)
    rt = jnp.where(lane == 0, i1, 0.0)
    rt = jnp.where(lane == 1, i2, rt)
    rt = jnp.where(lane == 2, r1, rt)
    rt = jnp.where(lane == 3, r2, rt)
    rt = jnp.where(lane == 4, g1, rt)
    rt = jnp.where(lane == 5, g2, rt)
    rt_ref[0] = rt


def _outproj(y_ssd, y_mla, y_gm, x, mod1, mod2, w_out, ln_g, ln_b, w_router=None, *, tm):
    nb, t, _ = x.shape
    route = w_router is not None
    row = lambda b, i: (b, i, 0)
    const2 = lambda b, i: (0, 0)
    modspec = pl.BlockSpec((1, 1, 3 * D_MODEL), lambda b, i: (b, 0, 0))
    in_specs = [
        pl.BlockSpec((1, tm, 512), row), pl.BlockSpec((1, tm, 256), row), pl.BlockSpec((1, tm, 256), row),
        pl.BlockSpec((1, tm, D_MODEL), row), modspec, modspec,
        pl.BlockSpec((D_MODEL, D_MODEL), const2),
        pl.BlockSpec((1, D_MODEL), const2), pl.BlockSpec((1, D_MODEL), const2),
    ]
    args = [y_ssd, y_mla, y_gm, x, mod1, mod2, w_out.astype(BF16), ln_g.reshape(1, -1), ln_b.reshape(1, -1)]
    out_specs = [pl.BlockSpec((1, tm, D_MODEL), row), pl.BlockSpec((1, tm, D_MODEL), row)]
    out_shape = [jax.ShapeDtypeStruct((nb, t, D_MODEL), F32),
                 jax.ShapeDtypeStruct((nb, t, D_MODEL), F32 if route else BF16)]
    scratch = []
    if route:
        wr = jnp.pad(w_router, ((0, 0), (0, LANES - N_EXPERTS))).astype(BF16)
        in_specs.append(pl.BlockSpec((D_MODEL, LANES), const2))
        args.append(wr)
        out_specs += [pl.BlockSpec((1, tm, LANES), row), pl.BlockSpec((8, LANES), const2)]
        out_shape += [jax.ShapeDtypeStruct((nb, t, LANES), F32), jax.ShapeDtypeStruct((8, LANES), F32)]
        scratch = [pltpu.VMEM((1, LANES), F32)]
    return pl.pallas_call(
        functools.partial(_outproj_kernel, route=route, tm=tm),
        grid=(nb, t // tm),
        in_specs=in_specs, out_specs=out_specs, out_shape=out_shape, scratch_shapes=scratch,
        compiler_params=_cparams(("arbitrary", "arbitrary")),
        name="outproj_route" if route else "outproj",
    )(*args)


def _swiglu_acc(h, w1_ref, w3_ref, w2_ref, widx, side_work=None):
    acc = None
    for j in range(D_FF // FF_CHUNK):
        lo = j * FF_CHUNK
        if side_work is not None:
            side_work(j)
        a = _dot(h, w1_ref[widx + (slice(None), slice(lo, lo + FF_CHUNK))])
        g = _dot(h, w3_ref[widx + (slice(None), slice(lo, lo + FF_CHUNK))])
        p = (_silu(a) * g).astype(BF16)
        part = _dot(p, w2_ref[widx + (slice(lo, lo + FF_CHUNK), slice(None))])
        acc = part if acc is None else acc + part
    return acc


def _ffn_kernel(h_ref, x_ref, mod_ref, w1_ref, w3_ref, w2_ref, g_ref, b_ref, o_ref):
    y = _swiglu_acc(h_ref[0], w1_ref, w3_ref, w2_ref, ())
    gate = mod_ref[0][:, 2 * D_MODEL:3 * D_MODEL]
    o_ref[0] = _residual_ln(x_ref[0], y, gate, g_ref[...], b_ref[...])


def _ffn(h, x, mod, w1, w3, w2, ln_g, ln_b, *, tm):
    nb, t, _ = x.shape
    row = lambda b, i: (b, i, 0)
    const2 = lambda b, i: (0, 0)
    wspec = lambda shape: pl.BlockSpec(shape, const2, pipeline_mode=pl.Buffered(1))
    return pl.pallas_call(
        _ffn_kernel,
        grid=(nb, t // tm),
        in_specs=[
            pl.BlockSpec((1, tm, D_MODEL), row), pl.BlockSpec((1, tm, D_MODEL), row),
            pl.BlockSpec((1, 1, 3 * D_MODEL), lambda b, i: (b, 0, 0)),
            wspec((D_MODEL, D_FF)), wspec((D_MODEL, D_FF)), wspec((D_FF, D_MODEL)),
            pl.BlockSpec((1, D_MODEL), const2), pl.BlockSpec((1, D_MODEL), const2),
        ],
        out_specs=pl.BlockSpec((1, tm, D_MODEL), row),
        out_shape=jax.ShapeDtypeStruct((nb, t, D_MODEL), F32),
        compiler_params=_cparams(("arbitrary", "arbitrary")),
        name="ffn_dense",
    )(h, x, mod, w1.astype(BF16), w3.astype(BF16), w2.astype(BF16), ln_g.reshape(1, -1), ln_b.reshape(1, -1))


def _invert_kernel(slot_ref, init_ref, inv_ref, sem, *, n_pairs):
    cp = pltpu.make_async_copy(init_ref, inv_ref, sem)
    cp.start()
    cp.wait()

    def body(i, c):
        inv_ref[slot_ref[i]] = i
        return c

    lax.fori_loop(0, n_pairs, body, 0, unroll=8)


def _moe_invert(slots, inv_init):
    return pl.pallas_call(
        functools.partial(_invert_kernel, n_pairs=slots.shape[0]),
        grid_spec=pltpu.PrefetchScalarGridSpec(
            num_scalar_prefetch=1,
            grid=(1,),
            in_specs=[pl.BlockSpec(memory_space=pl.ANY)],
            out_specs=pl.BlockSpec(memory_space=pltpu.SMEM),
            scratch_shapes=[pltpu.SemaphoreType.DMA(())],
        ),
        out_shape=jax.ShapeDtypeStruct(inv_init.shape, jnp.int32),
        compiler_params=_cparams(("arbitrary",)),
        name="moe_invert",
    )(slots, inv_init)


def _moe_ffn_kernel(te_ref, inv_ref, h_hbm, w1_ref, w3_ref, w2_ref, y_hbm, xbuf, obuf, gsem, ssem, *, tg, n_tok):
    del te_ref
    j = pl.program_id(0)
    last = pl.num_programs(0) - 1
    slot = lax.rem(j, 2)
    other = 1 - slot

    def gather_row(tile, r, sl):
        tok = inv_ref[(tile + 1) * tg + r] & (n_tok - 1)
        return pltpu.make_async_copy(h_hbm.at[pl.ds(tok, 1), :], xbuf.at[sl, pl.ds(r, 1), :], gsem.at[sl])

    def scatter_row(tile, r, sl):
        dst = inv_ref[(tile + 1) * tg + r]
        return pltpu.make_async_copy(obuf.at[sl, pl.ds(r, 1), :], y_hbm.at[pl.ds(dst, 1), :], ssem.at[sl])

    def wait_gather(sl):
        pltpu.make_async_copy(h_hbm.at[pl.ds(0, tg), :], xbuf.at[sl], gsem.at[sl]).wait()

    def wait_scatter(sl):
        pltpu.make_async_copy(obuf.at[sl], y_hbm.at[pl.ds(0, tg), :], ssem.at[sl]).wait()

    @pl.when(j == 0)
    def _():
        obuf[1] = jnp.zeros((tg, D_MODEL), F32)

        def first(r, c):
            gather_row(0, r, 0).start()
            return c

        lax.fori_loop(0, tg, first, 0)

    @pl.when(j > 0)
    def _():
        wait_scatter(slot)

    wait_gather(slot)
    x = xbuf[slot].astype(BF16)
    nxt = jnp.minimum(j + 1, last)
    n_chunks = D_FF // FF_CHUNK
    per = -(-tg // n_chunks)

    def side_work(c):
        for r in range(c * per, min((c + 1) * per, tg)):
            gather_row(nxt, r, other).start()
            scatter_row(j - 1, r, other).start(priority=1)

    obuf[slot] = _swiglu_acc(x, w1_ref, w3_ref, w2_ref, (0,), side_work)

    @pl.when(j == last)
    def _():
        wait_gather(other)
        wait_scatter(other)

        def final(r, c):
            scatter_row(j, r, slot).start()
            return c

        lax.fori_loop(0, tg, final, 0)
        wait_scatter(slot)


def _moe_ffn(tile_expert, inv, h, w1, w3, w2, *, tg, n_out):
    n_tiles = tile_expert.shape[0]
    wmap = lambda j, te, iv: (te[j], 0, 0)
    return pl.pallas_call(
        functools.partial(_moe_ffn_kernel, tg=tg, n_tok=h.shape[0]),
        grid_spec=pltpu.PrefetchScalarGridSpec(
            num_scalar_prefetch=2,
            grid=(n_tiles,),
            in_specs=[
                pl.BlockSpec(memory_space=pl.ANY),
                pl.BlockSpec((1, D_MODEL, D_FF), wmap),
                pl.BlockSpec((1, D_MODEL, D_FF), wmap),
                pl.BlockSpec((1, D_FF, D_MODEL), wmap),
            ],
            out_specs=pl.BlockSpec(memory_space=pl.ANY),
            scratch_shapes=[
                pltpu.VMEM((2, tg, D_MODEL), F32), pltpu.VMEM((2, tg, D_MODEL), F32),
                pltpu.SemaphoreType.DMA((2,)), pltpu.SemaphoreType.DMA((2,)),
            ],
        ),
        out_shape=jax.ShapeDtypeStruct((n_out, D_MODEL), F32),
        compiler_params=_cparams(("arbitrary",)),
        name="moe_ffn",
    )(tile_expert, inv, h, w1.astype(BF16), w3.astype(BF16), w2.astype(BF16))


def _combine_kernel(y1_ref, y2_ref, rt_ref, x_ref, mod_ref, g_ref, b_ref, o_ref):
    rt = rt_ref[0]
    y = rt[:, 4:5] * y1_ref[...] + rt[:, 5:6] * y2_ref[...]
    gate = mod_ref[0][:, 2 * D_MODEL:3 * D_MODEL]
    o_ref[0] = _residual_ln(x_ref[0], y, gate, g_ref[...], b_ref[...])


def _combine(y2, route, x, mod, ln_g, ln_b, *, tm):
    nb, t, _ = x.shape
    row = lambda b, i: (b, i, 0)
    const2 = lambda b, i: (0, 0)
    nt = t // tm
    return pl.pallas_call(
        _combine_kernel,
        grid=(nb, nt),
        in_specs=[
            pl.BlockSpec((tm, D_MODEL), lambda b, i: (b * nt + i, 0)),
            pl.BlockSpec((tm, D_MODEL), lambda b, i: (nb * nt + b * nt + i, 0)),
            pl.BlockSpec((1, tm, LANES), row),
            pl.BlockSpec((1, tm, D_MODEL), row),
            pl.BlockSpec((1, 1, 3 * D_MODEL), lambda b, i: (b, 0, 0)),
            pl.BlockSpec((1, D_MODEL), const2), pl.BlockSpec((1, D_MODEL), const2),
        ],
        out_specs=pl.BlockSpec((1, tm, D_MODEL), row),
        out_shape=jax.ShapeDtypeStruct((nb, t, D_MODEL), F32),
        compiler_params=_cparams(("arbitrary", "arbitrary")),
        name="moe_combine",
    )(y2, y2, route, x, mod, ln_g.reshape(1, -1), ln_b.reshape(1, -1))


def _moe_plan(route, counts, *, tg, n_tiles):
    n_pairs = 2 * route.shape[0] * route.shape[1]
    cnt = counts[0, :N_EXPERTS].astype(jnp.int32)
    tiles = (cnt + tg - 1) // tg
    tile_end = jnp.cumsum(tiles)
    start = (tile_end - tiles) * tg
    r = route.reshape(-1, LANES)
    e = r[:, 0:2].astype(jnp.int32)
    assert (n_pairs // 2) & (n_pairs // 2 - 1) == 0, "token count must be a power of two (pair ids are masked)"
    slots = jnp.transpose(start[e] + r[:, 2:4].astype(jnp.int32)).reshape(-1) + tg
    n_active = tile_end[-1]
    j = jnp.minimum(jnp.arange(n_tiles, dtype=jnp.int32), n_active - 1)
    tile_expert = jnp.sum((j[:, None] >= tile_end[None, :]).astype(jnp.int32), axis=1)
    s = jnp.arange(n_tiles * tg, dtype=jnp.int32)
    es = jnp.minimum(jnp.sum((s[:, None] >= (tile_end * tg)[None, :]).astype(jnp.int32), axis=1), N_EXPERTS - 1)
    pad_before = jnp.cumsum(tiles * tg - cnt) - (tiles * tg - cnt)
    in_group = s - start[es]
    pad_id = jnp.where(s >= n_active * tg, s - n_pairs, pad_before[es] + in_group - cnt[es])
    n_spare = n_tiles * tg - n_pairs
    body = n_pairs + jnp.clip(pad_id, 0, n_spare - 1)
    head = n_pairs + n_spare + jnp.arange(tg, dtype=jnp.int32)
    return slots, tile_expert, jnp.concatenate([head, body]), n_pairs + n_spare + tg


def _w_in_padded(w_in):
    z, xs, bm, cm = (0, 512), (512, 1024), (1024, 1152), (1152, 1280)
    dt, ql, kvl, kpe, gu, gv = (1280, 1288), (1288, 1544), (1544, 1672), (1672, 1704), (1704, 1960), (1960, 2216)
    seg = lambda ab: w_in[:, ab[0]:ab[1]]
    k_pe = seg(kpe)
    k_sw = jnp.concatenate([k_pe[:, MLA_ROPE // 2:], k_pe[:, :MLA_ROPE // 2]], axis=1)
    pad = jnp.zeros((D_MODEL, LANES - 2 * MLA_ROPE - SSD_HEADS), w_in.dtype)
    cols = [seg(z), seg(xs), seg(bm), seg(cm), seg(ql), seg(kvl), seg(gu), seg(gv), k_pe, k_sw, seg(dt), pad]
    return jnp.concatenate(cols, axis=1).astype(BF16)


def _forward(x, c, ln0_g, ln0_b, ada_w, ada_b, post_ln_g, post_ln_b, w_in, ssd_conv_w, ssd_conv_b,
             ssd_dt_bias, ssd_a_log, ssd_d, ssd_norm_w, mla_q_norm, mla_w_qb, mla_kv_norm, mla_w_kvb,
             gm_ln_g, gm_ln_b, gm_w_s, gm_b_s, w_out, ffn_w1, ffn_w3, ffn_w2,
             moe_router, moe_w1, moe_w3, moe_w2, *, tm, tq, tg, tc):
    nb, t, _ = x.shape
    mods = _modulations(c, ada_w, ada_b).reshape(2 * DEPTH, nb, 1, 3 * D_MODEL)
    tables = _rope_tables(t)
    for layer in range(DEPTH):
        mod_mix, mod_ffn = mods[2 * layer], mods[2 * layer + 1]
        w_pad = _w_in_padded(w_in[layer])
        if layer == 0:
            x, p_ssd, p_mla, p_gm, p_misc = _inproj(x, mod_mix, w_pad, ln0_g, ln0_b, tm=tm)
        else:
            p_ssd, p_mla, p_gm, p_misc = _inproj(x, mod_mix, w_pad, tm=tm)
        y_ssd = _ssd(p_ssd, p_misc, ssd_conv_w[layer], ssd_conv_b[layer], ssd_dt_bias[layer],
                     ssd_a_log[layer], ssd_d[layer], ssd_norm_w[layer], ts=tm)
        q, k, v = _mla_prep(p_mla, p_misc, mla_q_norm[layer], mla_kv_norm[layer], mla_w_qb[layer],
                            mla_w_kvb[layer], tables, tt=tm)
        y_mla = _flash(q, k, v, tq=tq)
        y_gm = _gmlp(p_gm, gm_ln_g[layer], gm_ln_b[layer], gm_w_s[layer], gm_b_s[layer], ts=tm)
        i = layer // 2
        if layer % 2 == 0:
            x, h = _outproj(y_ssd, y_mla, y_gm, x, mod_mix, mod_ffn, w_out[layer],
                            post_ln_g[layer, 0], post_ln_b[layer, 0], tm=tm)
            x = _ffn(h, x, mod_ffn, ffn_w1[i], ffn_w3[i], ffn_w2[i], post_ln_g[layer, 1], post_ln_b[layer, 1], tm=tm)
        else:
            x, h, route, counts = _outproj(y_ssd, y_mla, y_gm, x, mod_mix, mod_ffn, w_out[layer],
                                           post_ln_g[layer, 0], post_ln_b[layer, 0], moe_router[i], tm=tm)
            n_tiles = (2 * nb * t) // tg + N_EXPERTS
            slots, tile_expert, inv_init, n_out = _moe_plan(route, counts, tg=tg, n_tiles=n_tiles)
            inv = _moe_invert(slots, inv_init)
            y2 = _moe_ffn(tile_expert, inv, h.reshape(nb * t, D_MODEL), moe_w1[i], moe_w3[i], moe_w2[i],
                          tg=tg, n_out=n_out)
            x = _combine(y2, route, x, mod_ffn, post_ln_g[layer, 1], post_ln_b[layer, 1], tm=tc)
    return x


def kernel(x, c, ln0_g, ln0_b, ada_w, ada_b, post_ln_g, post_ln_b, w_in, ssd_conv_w, ssd_conv_b, ssd_dt_bias, ssd_a_log, ssd_d, ssd_norm_w, mla_q_norm, mla_w_qb, mla_kv_norm, mla_w_kvb, gm_ln_g, gm_ln_b, gm_w_s, gm_b_s, w_out, ffn_w1, ffn_w3, ffn_w2, moe_router, moe_w1, moe_w3, moe_w2):
    t = x.shape[1]
    tm = min(1024, t)
    return _forward(x, c, ln0_g, ln0_b, ada_w, ada_b, post_ln_g, post_ln_b, w_in, ssd_conv_w, ssd_conv_b,
                    ssd_dt_bias, ssd_a_log, ssd_d, ssd_norm_w, mla_q_norm, mla_w_qb, mla_kv_norm, mla_w_kvb,
                    gm_ln_g, gm_ln_b, gm_w_s, gm_b_s, w_out, ffn_w1, ffn_w3, ffn_w2,
                    moe_router, moe_w1, moe_w3, moe_w2, tm=tm, tq=min(1024, t), tg=min(512, t), tc=min(512, t))
```
